```python
import jax, jax.numpy as jnp
from jax import lax
import numpy as np

D_MODEL = 2048
BATCH = 4
SEQ = 2048
DEPTH = 1
DEC_BATCH = 32
DEC_SEQ = 1
PAST_LEN = 8192
PAGE_SIZE = 128

GLA_HEADS = 4
GLA_DK = D_MODEL // 8
GLA_DV = D_MODEL // 4
GLA_QK_W = GLA_HEADS * GLA_DK
GLA_V_W = GLA_HEADS * GLA_DV
GLA_RANK = 16
GLA_TAU = 16.0
GLA_CHUNK = 64
FOX_DH = 128
FOX_HEADS = D_MODEL // FOX_DH
FOX_W = FOX_HEADS * FOX_DH
FOX_Q_BLOCK = 128
N_GROUPS = 4
EXPERTS_PER_GROUP = 8
N_EXPERTS = N_GROUPS * EXPERTS_PER_GROUP
TOP_K_IN_GROUP = 2
EXPERT_FF = D_MODEL // 2
MOE_BLOCK = 128
PLE_DIM = 256
RMS_EPS = 1e-6
IN_SIZES = (GLA_QK_W, GLA_QK_W, GLA_V_W, GLA_V_W, GLA_RANK, FOX_W, FOX_W, FOX_W, FOX_HEADS, D_MODEL, D_MODEL)
IN_W = 2 * GLA_QK_W + 2 * GLA_V_W + GLA_RANK + 3 * FOX_W + FOX_HEADS + 2 * D_MODEL

kernel_name = 'gla_fox_hmoe_decoder_step'


def rms_norm(x, g):
    xf = x.astype(jnp.float32)
    y = xf * lax.rsqrt(jnp.mean(xf * xf, axis=-1, keepdims=True) + RMS_EPS)
    return (y * g.astype(jnp.float32)).astype(x.dtype)


def split_columns(z):
    offsets, acc = [], 0
    for size in IN_SIZES[:-1]:
        acc += size
        offsets.append(acc)
    return jnp.split(z, offsets, axis=-1)


def mixer_inputs(u, w_in, w_a2, b_a, b_f):
    B, S, _ = u.shape
    gq, gk, gv, gr, ga, fq, fk, fv, ff, mg_gla, mg_fox = split_columns(u @ w_in)
    gq = gq.reshape(B, S, GLA_HEADS, GLA_DK) * (GLA_DK ** -0.5)
    gk = gk.reshape(B, S, GLA_HEADS, GLA_DK)
    gv = gv.reshape(B, S, GLA_HEADS, GLA_DV)
    la = (jax.nn.log_sigmoid((ga @ w_a2 + b_a).astype(jnp.float32)) / GLA_TAU).reshape(B, S, GLA_HEADS, GLA_DK)
    fq = fq.reshape(B, S, FOX_HEADS, FOX_DH)
    fk = fk.reshape(B, S, FOX_HEADS, FOX_DH)
    fv = fv.reshape(B, S, FOX_HEADS, FOX_DH)
    logf = jax.nn.log_sigmoid((ff + b_f).astype(jnp.float32))
    return (gq, gk, gv, gr, la), (fq, fk, fv, logf), (mg_gla, mg_fox)


def gla_block(S, q, k, v, la):
    f32 = jnp.float32
    q, k, v, la = q.astype(f32), k.astype(f32), v.astype(f32), la.astype(f32)
    C = q.shape[1]
    b = jnp.cumsum(la, axis=1)
    o_inter = jnp.einsum('bthk,bhkv->bthv', q * jnp.exp(b), S)
    causal = jnp.tril(jnp.ones((C, C), dtype=bool))
    diff = b[:, :, None] - b[:, None, :]
    decay = jnp.exp(jnp.where(causal[None, :, :, None, None], diff, -jnp.inf))
    A = jnp.sum(q[:, :, None] * k[:, None, :] * decay, axis=-1)
    o_intra = jnp.einsum('btsh,bshv->bthv', A, v)
    b_last = b[:, -1]
    S_new = jnp.exp(b_last)[..., None] * S + jnp.einsum('bshk,bshv->bhkv', k * jnp.exp(b_last[:, None] - b), v)
    return o_inter + o_intra, S_new


def gla_prompt(q, k, v, la):
    B, S, H, DK = q.shape
    DV = v.shape[-1]
    nc = S // GLA_CHUNK

    def to_chunks(a):
        return jnp.moveaxis(a.reshape((B, nc, GLA_CHUNK) + a.shape[2:]), 1, 0)

    def step(state, c):
        o, state = gla_block(state, *c)
        return state, o

    S0 = jnp.zeros((B, H, DK, DV), jnp.float32)
    S_fin, o = lax.scan(step, S0, (to_chunks(q), to_chunks(k), to_chunks(v), to_chunks(la)))
    return jnp.moveaxis(o, 0, 1).reshape(B, S, H, DV), S_fin


def fox_prompt(q, k, v, logf):
    B, S, H, Dh = q.shape
    nb = S // FOX_Q_BLOCK
    c = jnp.cumsum(logf, axis=1).transpose(0, 2, 1)
    q_blocks = jnp.moveaxis(q.reshape(B, nb, FOX_Q_BLOCK, H, Dh), 1, 0)
    c_blocks = jnp.moveaxis(c.reshape(B, H, nb, FOX_Q_BLOCK), 2, 0)
    key_pos = jnp.arange(S)
    scale = Dh ** -0.5

    def one_block(args):
        blk, qb, cb = args
        s = jnp.einsum('bqhd,bkhd->bhqk', qb, k).astype(jnp.float32) * scale
        s = s + (cb[..., :, None] - c[:, :, None, :])
        q_pos = blk * FOX_Q_BLOCK + jnp.arange(FOX_Q_BLOCK)
        s = jnp.where(key_pos[None, :] <= q_pos[:, None], s, -jnp.inf)
        p = jax.nn.softmax(s, axis=-1)
        return jnp.einsum('bhqk,bkhd->bqhd', p.astype(v.dtype), v)

    o = lax.map(one_block, (jnp.arange(nb), q_blocks, c_blocks))
    return jnp.moveaxis(o, 0, 1).reshape(B, S, H, Dh)


def fox_sample(q, k, v, logf, pool_k, pool_v, pool_lf, page_table):
    B, Sd, H, Dh = q.shape
    k_past = pool_k[page_table].reshape(B, -1, H, Dh)
    v_past = pool_v[page_table].reshape(B, -1, H, Dh)
    P = k_past.shape[1]
    c_past = jnp.cumsum(pool_lf[page_table].reshape(B, P, H).astype(jnp.float32), axis=1).transpose(0, 2, 1)
    c_new = c_past[:, :, -1:] + jnp.cumsum(logf, axis=1).transpose(0, 2, 1)
    scale = Dh ** -0.5
    s_past = jnp.einsum('bqhd,bkhd->bhqk', q, k_past).astype(jnp.float32) * scale
    s_past = s_past + (c_new[..., :, None] - c_past[:, :, None, :])
    s_new = jnp.einsum('bqhd,bkhd->bhqk', q, k).astype(jnp.float32) * scale
    s_new = s_new + (c_new[..., :, None] - c_new[:, :, None, :])
    s_new = jnp.where(jnp.tril(jnp.ones((Sd, Sd), dtype=bool)), s_new, -jnp.inf)
    p = jax.nn.softmax(jnp.concatenate([s_past, s_new], axis=-1), axis=-1)
    o = jnp.einsum('bhqk,bkhd->bqhd', p[..., :P].astype(v.dtype), v_past)
    return o + jnp.einsum('bhqk,bkhd->bqhd', p[..., P:].astype(v.dtype), v)


def merge_branches(o_gla, r, o_fox, gates, g_gla_out, w_o_gla, w_o_fox, w_out):
    B, S = r.shape[:2]
    mg_gla, mg_fox = gates
    o_gla = rms_norm(o_gla, g_gla_out).astype(r.dtype).reshape(B, S, GLA_V_W) * jax.nn.silu(r)
    o_fox = o_fox.reshape(B, S, FOX_W)
    mix = jax.nn.sigmoid(mg_gla) * (o_gla @ w_o_gla) + jax.nn.sigmoid(mg_fox) * (o_fox @ w_o_fox)
    return mix @ w_out


def grouped_experts(t, expert, weight, w_gate, w_up, w_down):
    T, D = t.shape
    n_slots = T * TOP_K_IN_GROUP
    flat_e = expert.reshape(-1)
    order = jnp.argsort(flat_e)
    sorted_e = flat_e[order]
    tok = order // TOP_K_IN_GROUP
    counts = jnp.bincount(flat_e, length=N_EXPERTS).astype(jnp.int32)
    padded = ((counts + MOE_BLOCK - 1) // MOE_BLOCK) * MOE_BLOCK
    cum_counts = jnp.cumsum(counts)
    cum_padded = jnp.cumsum(padded)
    rank = jnp.arange(n_slots, dtype=jnp.int32) - (cum_counts - counts)[sorted_e]
    dest = (cum_padded - padded)[sorted_e] + rank
    n_blocks = -(-n_slots // MOE_BLOCK) + N_EXPERTS
    n_rows = n_blocks * MOE_BLOCK
    xp = jnp.zeros((n_rows, D), t.dtype).at[dest].set(t[tok])
    block_start = jnp.arange(n_blocks, dtype=jnp.int32) * MOE_BLOCK
    block_expert = jnp.clip(jnp.searchsorted(cum_padded, block_start, side='right'), 0, N_EXPERTS - 1)

    def run_block(args):
        xb, e = args
        return (jax.nn.silu(xb @ w_gate[e]) * (xb @ w_up[e])) @ w_down[e]

    yp = lax.map(run_block, (xp.reshape(n_blocks, MOE_BLOCK, D), block_expert)).reshape(n_rows, D)
    ys = (yp[dest] * weight.reshape(-1)[order][:, None]).astype(t.dtype)
    return jnp.zeros_like(t).at[tok].add(ys)


def hier_moe(u, w_rg, b_rg, w_re, b_re, w_gate, w_up, w_down):
    B, S, D = u.shape
    t = u.reshape(-1, D)
    T = t.shape[0]
    grp_prob = jax.nn.softmax((t @ w_rg + b_rg).astype(jnp.float32), axis=-1)
    g_w, g_idx = lax.top_k(grp_prob, 1)
    exp_logits = (t @ w_re + b_re).astype(jnp.float32).reshape(T, N_GROUPS, EXPERTS_PER_GROUP)
    in_grp = jnp.take_along_axis(exp_logits, g_idx[:, :, None], axis=1)[:, 0]
    e_w, e_idx = lax.top_k(jax.nn.softmax(in_grp, axis=-1), TOP_K_IN_GROUP)
    weight = g_w * e_w / jnp.sum(e_w, axis=-1, keepdims=True)
    expert = g_idx * EXPERTS_PER_GROUP + e_idx
    return grouped_experts(t, expert, weight, w_gate, w_up, w_down).reshape(B, S, D)


def channel_and_ple(h, p, g_ffn, w_rg, b_rg, w_re, b_re, w_gate, w_up, w_down, g_ple, w_ple_gate, w_ple_proj):
    h = h + hier_moe(rms_norm(h, g_ffn), w_rg, b_rg, w_re, b_re, w_gate, w_up, w_down)
    gate = jax.nn.sigmoid(rms_norm(h, g_ple) @ w_ple_gate)
    return h + gate * (p.astype(h.dtype) @ w_ple_proj)


def setup_inputs(seed: int = 0) -> dict:
    key = jax.random.key(seed)
    ks = jax.random.split(key, 40)
    f32 = jnp.float32

    def nrm(k, shape, scale):
        return jax.random.normal(k, shape, f32) * scale

    n_pages = PAST_LEN // PAGE_SIZE
    n_used = DEC_BATCH * n_pages
    n_phys = n_used + n_used // 4
    page_table = jax.random.permutation(ks[0], n_phys)[:n_used].reshape(DEC_BATCH, n_pages).astype(jnp.int32)
    return {
        'x_prompt': nrm(ks[1], (BATCH, SEQ, D_MODEL), 1.0),
        'x_sample': nrm(ks[2], (DEC_BATCH, DEC_SEQ, D_MODEL), 1.0),
        'cache_fox_k': nrm(ks[3], (DEPTH, n_phys, PAGE_SIZE, FOX_HEADS, FOX_DH), 1.0),
        'cache_fox_v': nrm(ks[4], (DEPTH, n_phys, PAGE_SIZE, FOX_HEADS, FOX_DH), 1.0),
        'cache_fox_logf': jax.nn.log_sigmoid(4.0 + nrm(ks[5], (DEPTH, n_phys, PAGE_SIZE, FOX_HEADS), 1.0)),
        'state_gla': nrm(ks[6], (DEPTH, DEC_BATCH, GLA_HEADS, GLA_DK, GLA_DV), 1.0),
        'page_table': page_table,
        'p_prompt': nrm(ks[7], (DEPTH, BATCH, SEQ, PLE_DIM), 1.0),
        'p_sample': nrm(ks[8], (DEPTH, DEC_BATCH, DEC_SEQ, PLE_DIM), 1.0),
        'g_mix': 1.0 + nrm(ks[9], (DEPTH, D_MODEL), 0.05),
        'w_in': nrm(ks[10], (DEPTH, D_MODEL, IN_W), D_MODEL ** -0.5),
        'w_gla_a2': nrm(ks[11], (DEPTH, GLA_RANK, GLA_QK_W), GLA_RANK ** -0.5),
        'b_gla_a': nrm(ks[12], (DEPTH, GLA_QK_W), 0.1),
        'b_fox_f': 4.0 + nrm(ks[13], (DEPTH, FOX_HEADS), 0.5),
        'g_gla_out': 1.0 + nrm(ks[14], (DEPTH, GLA_DV), 0.05),
        'w_o_gla': nrm(ks[15], (DEPTH, GLA_V_W, D_MODEL), GLA_V_W ** -0.5),
        'w_o_fox': nrm(ks[16], (DEPTH, FOX_W, D_MODEL), FOX_W ** -0.5),
        'w_out': nrm(ks[17], (DEPTH, D_MODEL, D_MODEL), D_MODEL ** -0.5),
        'g_ffn': 1.0 + nrm(ks[18], (DEPTH, D_MODEL), 0.05),
        'w_route_group': nrm(ks[19], (DEPTH, D_MODEL, N_GROUPS), D_MODEL ** -0.5),
        'b_route_group': nrm(ks[20], (DEPTH, N_GROUPS), 0.01),
        'w_route_expert': nrm(ks[21], (DEPTH, D_MODEL, N_EXPERTS), D_MODEL ** -0.5),
        'b_route_expert': nrm(ks[22], (DEPTH, N_EXPERTS), 0.01),
        'w_exp_gate': nrm(ks[23], (DEPTH, N_EXPERTS, D_MODEL, EXPERT_FF), D_MODEL ** -0.5),
        'w_exp_up': nrm(ks[24], (DEPTH, N_EXPERTS, D_MODEL, EXPERT_FF), D_MODEL ** -0.5),
        'w_exp_down': nrm(ks[25], (DEPTH, N_EXPERTS, EXPERT_FF, D_MODEL), EXPERT_FF ** -0.5),
        'g_ple': 1.0 + nrm(ks[26], (DEPTH, D_MODEL), 0.05),
        'w_ple_gate': nrm(ks[27], (DEPTH, D_MODEL, D_MODEL), D_MODEL ** -0.5),
        'w_ple_proj': nrm(ks[28], (DEPTH, PLE_DIM, D_MODEL), PLE_DIM ** -0.5),
        'g_final': 1.0 + nrm(ks[29], (D_MODEL,), 0.05),
    }


def reference(x_prompt, x_sample, cache_fox_k, cache_fox_v, cache_fox_logf, state_gla, page_table,
              p_prompt, p_sample, g_mix, w_in, w_gla_a2, b_gla_a, b_fox_f, g_gla_out, w_o_gla, w_o_fox, w_out,
              g_ffn, w_route_group, b_route_group, w_route_expert, b_route_expert, w_exp_gate, w_exp_up,
              w_exp_down, g_ple, w_ple_gate, w_ple_proj, g_final):
    hp, hs = x_prompt, x_sample
    kp_l, vp_l, lfp_l, sp_l = [], [], [], []
    ks_l, vs_l, lfs_l, ss_l = [], [], [], []
    for i in range(DEPTH):
        ffn = (g_ffn[i], w_route_group[i], b_route_group[i], w_route_expert[i], b_route_expert[i],
               w_exp_gate[i], w_exp_up[i], w_exp_down[i], g_ple[i], w_ple_gate[i], w_ple_proj[i])
        u = rms_norm(hp, g_mix[i])
        (gq, gk, gv, gr, la), (fq, fk, fv, lf), gates = mixer_inputs(u, w_in[i], w_gla_a2[i], b_gla_a[i], b_fox_f[i])
        o_gla, s_gla = gla_prompt(gq, gk, gv, la)
        o_fox = fox_prompt(fq, fk, fv, lf)
        hp = hp + merge_branches(o_gla, gr, o_fox, gates, g_gla_out[i], w_o_gla[i], w_o_fox[i], w_out[i])
        hp = channel_and_ple(hp, p_prompt[i], *ffn)
        kp_l.append(fk.astype(cache_fox_k.dtype))
        vp_l.append(fv.astype(cache_fox_v.dtype))
        lfp_l.append(lf.astype(cache_fox_logf.dtype))
        sp_l.append(s_gla.astype(state_gla.dtype))
        u = rms_norm(hs, g_mix[i])
        (gq, gk, gv, gr, la), (fq, fk, fv, lf), gates = mixer_inputs(u, w_in[i], w_gla_a2[i], b_gla_a[i], b_fox_f[i])
        o_gla, s_gla = gla_block(state_gla[i].astype(jnp.float32), gq, gk, gv, la)
        o_fox = fox_sample(fq, fk, fv, lf, cache_fox_k[i], cache_fox_v[i], cache_fox_logf[i], page_table)
        hs = hs + merge_branches(o_gla, gr, o_fox, gates, g_gla_out[i], w_o_gla[i], w_o_fox[i], w_out[i])
        hs = channel_and_ple(hs, p_sample[i], *ffn)
        ks_l.append(fk.astype(cache_fox_k.dtype))
        vs_l.append(fv.astype(cache_fox_v.dtype))
        lfs_l.append(lf.astype(cache_fox_logf.dtype))
        ss_l.append(s_gla.astype(state_gla.dtype))
    y_prompt = rms_norm(hp, g_final)
    y_sample = rms_norm(hs, g_final)
    return (y_prompt, y_sample,
            jnp.stack(kp_l), jnp.stack(vp_l), jnp.stack(lfp_l), jnp.stack(sp_l),
            jnp.stack(ks_l), jnp.stack(vs_l), jnp.stack(lfs_l), jnp.stack(ss_l))
```

```python
import functools

import jax
import jax.numpy as jnp
from jax import lax
from jax.experimental import pallas as pl
from jax.experimental.pallas import tpu as pltpu

F32 = jnp.float32
BF16 = jnp.bfloat16
I32 = jnp.int32
HIGHEST = lax.Precision.HIGHEST
NEG_INF = float("-inf")

RMS_EPS = 1e-6
GLA_HEADS = 4
GLA_RANK = 16
GLA_TAU = 16.0
GLA_CHUNK = 64
GLA_SUB = 16
FOX_DH = 128
N_GROUPS = 4
EXPERTS_PER_GROUP = 8
N_EXPERTS = N_GROUPS * EXPERTS_PER_GROUP
TOP_K = 2

LANES = 128
MOE_ROWS = 256
VMEM_LIMIT = 56 * 2**20


def _params(*sem):
    return pltpu.CompilerParams(dimension_semantics=sem, vmem_limit_bytes=VMEM_LIMIT)


def _log_sigmoid(x):
    return jnp.minimum(x, 0.0) - jnp.log1p(jnp.exp(-jnp.abs(x)))


def _rms(x, g):
    return x * lax.rsqrt(jnp.mean(x * x, axis=-1, keepdims=True) + RMS_EPS) * g


def _nt_dot(a, b):
    return lax.dot_general(a, b, (((1,), (1,)), ((), ())), preferred_element_type=F32)


def _tn_dot(a, b):
    return lax.dot_general(a, b, (((0,), (0,)), ((), ())), preferred_element_type=F32)


def _row_to_col(row):
    n = row.shape[1]
    eye = lax.broadcasted_iota(I32, (n, n), 0) == lax.broadcasted_iota(I32, (n, n), 1)
    return jnp.sum(jnp.where(eye, row, 0.0), axis=1, keepdims=True)


def _rms_cast_kernel(x_ref, g_ref, o_ref):
    o_ref[...] = _rms(x_ref[...], g_ref[...]).astype(o_ref.dtype)


def rms_cast(x, g):
    t, d = x.shape
    tm = min(512, t)
    return pl.pallas_call(
        _rms_cast_kernel,
        grid=(t // tm,),
        in_specs=[pl.BlockSpec((tm, d), lambda i: (i, 0)), pl.BlockSpec((1, d), lambda i: (0, 0))],
        out_specs=pl.BlockSpec((tm, d), lambda i: (i, 0)),
        out_shape=jax.ShapeDtypeStruct((t, d), BF16),
        compiler_params=_params("parallel"),
        name="rms_cast",
    )(x, g.reshape(1, d))


def _mm_kernel(a_ref, w_ref, o_ref):
    o_ref[...] = jnp.dot(a_ref[...], w_ref[...], preferred_element_type=F32).astype(o_ref.dtype)


def matmul(a, w, out_dtype=F32, tm=512, tn=1024):
    t, k = a.shape
    n = w.shape[1]
    tm, tn = min(tm, t), min(tn, n)
    return pl.pallas_call(
        _mm_kernel,
        grid=(n // tn, t // tm),
        in_specs=[pl.BlockSpec((tm, k), lambda j, i: (i, 0)), pl.BlockSpec((k, tn), lambda j, i: (0, j))],
        out_specs=pl.BlockSpec((tm, tn), lambda j, i: (i, j)),
        out_shape=jax.ShapeDtypeStruct((t, n), out_dtype),
        compiler_params=_params("parallel", "parallel"),
        name="matmul",
    )(a, w)


def _la_kernel(z_ref, w_ref, b_ref, o_ref):
    pre = jnp.dot(z_ref[...], w_ref[...], precision=HIGHEST, preferred_element_type=F32) + b_ref[...]
    o_ref[...] = _log_sigmoid(pre) / GLA_TAU


def gla_log_decay(z_small, w_a2_pad, b_a):
    t = z_small.shape[0]
    n = w_a2_pad.shape[1]
    tm = min(512, t)
    return pl.pallas_call(
        _la_kernel,
        grid=(t // tm,),
        in_specs=[pl.BlockSpec((tm, LANES), lambda i: (i, 0)),
                  pl.BlockSpec((LANES, n), lambda i: (0, 0)),
                  pl.BlockSpec((1, n), lambda i: (0, 0))],
        out_specs=pl.BlockSpec((tm, n), lambda i: (i, 0)),
        out_shape=jax.ShapeDtypeStruct((t, n), F32),
        compiler_params=_params("parallel"),
        name="gla_log_decay",
    )(z_small, w_a2_pad, b_a.reshape(1, n))


def _logf_kernel(f_ref, b_ref, lf_ref, c_ref, *, blk):
    lf = _log_sigmoid(f_ref[...] + b_ref[...])
    lf_ref[...] = lf
    s = lf.shape[1]
    upper = (lax.broadcasted_iota(I32, (blk, blk), 0) <= lax.broadcasted_iota(I32, (blk, blk), 1)).astype(F32)
    carry = jnp.zeros((lf.shape[0], 1), F32)
    for j in range(s // blk):
        cs = jnp.dot(lf[:, j * blk:(j + 1) * blk], upper, precision=HIGHEST, preferred_element_type=F32) + carry
        c_ref[:, j * blk:(j + 1) * blk] = cs
        carry = cs[:, blk - 1:blk]


def fox_log_forget(f_t, b_f):
    b, h, s = f_t.shape
    blk = min(256, s)
    spec = pl.BlockSpec((None, h, s), lambda i: (i, 0, 0))
    return pl.pallas_call(
        functools.partial(_logf_kernel, blk=blk),
        grid=(b,),
        in_specs=[spec, pl.BlockSpec((h, 1), lambda i: (0, 0))],
        out_specs=[spec, spec],
        out_shape=[jax.ShapeDtypeStruct((b, h, s), F32)] * 2,
        compiler_params=_params("parallel"),
        name="fox_log_forget",
    )(f_t, b_f.reshape(h, 1))


def _gla_prompt_kernel(q_ref, k_ref, v_ref, r_ref, la_ref, g_ref, o_ref, s_ref, *, rows):
    @pl.when(pl.program_id(2) == 0)
    def _():
        s_ref[...] = jnp.zeros_like(s_ref)

    c, sub = GLA_CHUNK, GLA_SUB
    nsub = c // sub
    dk = q_ref.shape[1]
    scale = dk ** -0.5
    row = lax.broadcasted_iota(I32, (c, c), 0)
    col = lax.broadcasted_iota(I32, (c, c), 1)
    lower = (row >= col).astype(F32)
    col_sub = lax.broadcasted_iota(I32, (sub, c), 1)
    row_in_sub = lax.broadcasted_iota(I32, (c, 1), 0) % sub

    def sub_rows(x, s):
        return jnp.concatenate(
            [jnp.broadcast_to(x[i * sub + s:i * sub + s + 1, :], (sub, dk)) for i in range(nsub)], axis=0)

    def chunk(ci, carry):
        sl = pl.ds(pl.multiple_of(ci * c, c), c)
        q = q_ref[sl, :] * scale
        k = k_ref[sl, :]
        vb = v_ref[sl, :].astype(BF16)
        b = jnp.dot(lower, la_ref[sl, :], precision=HIGHEST, preferred_element_type=F32)
        state = s_ref[...]

        o = jnp.dot((q * jnp.exp(b)).astype(BF16), state.astype(BF16), preferred_element_type=F32)

        a = jnp.zeros((c, c), F32)
        for s in range(sub):
            e = jnp.exp(jnp.where(row_in_sub >= s, b - sub_rows(b, s), NEG_INF))
            w = jnp.sum(q * e * sub_rows(k, s), axis=-1, keepdims=True)
            a = jnp.where(col == (row // sub) * sub + s, w, a)
        blocks = [a[:sub]]
        for i in range(1, nsub):
            ri = slice(i * sub, (i + 1) * sub)
            edge = b[i * sub - 1:i * sub, :]
            qs = (q[ri] * jnp.exp(b[ri] - edge)).astype(BF16)
            ks = (k * jnp.exp(jnp.minimum(edge - b, 0.0))).astype(BF16)
            blocks.append(jnp.where(col_sub < i * sub, _nt_dot(qs, ks), a[ri]))
        a = jnp.concatenate(blocks, axis=0)
        o = o + jnp.dot(a.astype(BF16), vb, preferred_element_type=F32)

        r = r_ref[sl, :]
        o_ref[sl, :] = (_rms(o, g_ref[...]) * (r * jax.nn.sigmoid(r))).astype(o_ref.dtype)

        last = b[c - 1:c, :]
        kd = (k * jnp.exp(last - b)).astype(BF16)
        s_ref[...] = _row_to_col(jnp.exp(last)) * state + _tn_dot(kd, vb)
        return carry

    lax.fori_loop(0, rows // c, chunk, 0)


def gla_prompt(z, la, g_out, batch, seq, off):
    t = z.shape[0]
    dk = la.shape[1] // GLA_HEADS
    dv = g_out.shape[0]
    rows = min(512, seq)
    nt = seq // rows
    kern = functools.partial(_gla_prompt_kernel, rows=rows)
    rmap = lambda b, h, i: b * nt + i
    return pl.pallas_call(
        kern,
        grid=(batch, GLA_HEADS, nt),
        in_specs=[
            pl.BlockSpec((rows, dk), lambda b, h, i: (rmap(b, h, i), off["gq"] // dk + h)),
            pl.BlockSpec((rows, dk), lambda b, h, i: (rmap(b, h, i), off["gk"] // dk + h)),
            pl.BlockSpec((rows, dv), lambda b, h, i: (rmap(b, h, i), off["gv"] // dv + h)),
            pl.BlockSpec((rows, dv), lambda b, h, i: (rmap(b, h, i), off["gr"] // dv + h)),
            pl.BlockSpec((rows, dk), lambda b, h, i: (rmap(b, h, i), h)),
            pl.BlockSpec((1, dv), lambda b, h, i: (0, 0)),
        ],
        out_specs=[
            pl.BlockSpec((rows, dv), lambda b, h, i: (rmap(b, h, i), h)),
            pl.BlockSpec((None, None, dk, dv), lambda b, h, i: (b, h, 0, 0)),
        ],
        out_shape=[jax.ShapeDtypeStruct((t, GLA_HEADS * dv), BF16),
                   jax.ShapeDtypeStruct((batch, GLA_HEADS, dk, dv), F32)],
        compiler_params=_params("parallel", "parallel", "arbitrary"),
        name="gla_prompt",
    )(z, z, z, z, la, g_out.reshape(1, dv))


def _gla_sample_kernel(q_ref, k_ref, v_ref, r_ref, la_ref, g_ref, s_ref, o_ref, sn_ref):
    dk = q_ref.shape[1]
    q = _row_to_col(q_ref[...] * dk ** -0.5)
    k = _row_to_col(k_ref[...])
    decay = _row_to_col(jnp.exp(la_ref[...]))
    new = decay * s_ref[...] + k * v_ref[...]
    sn_ref[...] = new
    o = jnp.sum(q * new, axis=0, keepdims=True)
    r = r_ref[...]
    o_ref[...] = (_rms(o, g_ref[...]) * (r * jax.nn.sigmoid(r))).astype(o_ref.dtype)


def gla_sample(z, la, g_out, state, off):
    b = z.shape[0]
    _, h, dk, dv = state.shape
    z3 = z.reshape(b, 1, z.shape[1])
    la3 = la.reshape(b, 1, la.shape[1])
    og, sn = pl.pallas_call(
        _gla_sample_kernel,
        grid=(b, h),
        in_specs=[
            pl.BlockSpec((None, 1, dk), lambda i, j: (i, 0, off["gq"] // dk + j)),
            pl.BlockSpec((None, 1, dk), lambda i, j: (i, 0, off["gk"] // dk + j)),
            pl.BlockSpec((None, 1, dv), lambda i, j: (i, 0, off["gv"] // dv + j)),
            pl.BlockSpec((None, 1, dv), lambda i, j: (i, 0, off["gr"] // dv + j)),
            pl.BlockSpec((None, 1, dk), lambda i, j: (i, 0, j)),
            pl.BlockSpec((1, dv), lambda i, j: (0, 0)),
            pl.BlockSpec((None, None, dk, dv), lambda i, j: (i, j, 0, 0)),
        ],
        out_specs=[
            pl.BlockSpec((None, 1, dv), lambda i, j: (i, 0, j)),
            pl.BlockSpec((None, None, dk, dv), lambda i, j: (i, j, 0, 0)),
        ],
        out_shape=[jax.ShapeDtypeStruct((b, 1, h * dv), BF16), jax.ShapeDtypeStruct(state.shape, F32)],
        compiler_params=_params("parallel", "parallel"),
        name="gla_sample",
    )(z3, z3, z3, z3, la3, g_out.reshape(1, dv), state)
    return og.reshape(b, h * dv), sn


def _fox_prompt_kernel(q_ref, k_ref, v_ref, c_ref, o_ref, qb_ref, kb_ref, vb_ref, *, blk):
    s, dh = q_ref.shape
    qb_ref[...] = (q_ref[...] * dh ** -0.5).astype(BF16)
    kb_ref[...] = k_ref[...].astype(BF16)
    vb_ref[...] = v_ref[...].astype(BF16)
    causal = lax.broadcasted_iota(I32, (blk, blk), 0) >= lax.broadcasted_iota(I32, (blk, blk), 1)

    def update(q, j, carry, diagonal):
        m, l, acc = carry
        ks = pl.ds(pl.multiple_of(j * blk, blk), blk)
        logit = _nt_dot(q, kb_ref[ks, :]) - c_ref[:, ks]
        if diagonal:
            logit = jnp.where(causal, logit, NEG_INF)
        m_new = jnp.maximum(m, jnp.max(logit, axis=-1, keepdims=True))
        alpha = jnp.exp(m - m_new)
        p = jnp.exp(logit - m_new)
        l = alpha * l + jnp.sum(p, axis=-1, keepdims=True)
        acc = alpha * acc + jnp.dot(p.astype(BF16), vb_ref[ks, :], preferred_element_type=F32)
        return m_new, l, acc

    def q_block(i, _):
        qs = pl.ds(pl.multiple_of(i * blk, blk), blk)
        q = qb_ref[qs, :]
        init = (jnp.full((blk, 1), NEG_INF, F32), jnp.zeros((blk, 1), F32), jnp.zeros((blk, dh), F32))
        carry = lax.fori_loop(0, i, lambda j, cr: update(q, j, cr, False), init)
        _, l, acc = update(q, i, carry, True)
        o_ref[qs, :] = (acc / l).astype(o_ref.dtype)
        return 0

    lax.fori_loop(0, s // blk, q_block, 0)


def fox_prompt(z, fk, fv, c_t, batch, seq, q_off):
    t, w = fk.shape
    h = w // FOX_DH
    blk = min(256, seq)
    return pl.pallas_call(
        functools.partial(_fox_prompt_kernel, blk=blk),
        grid=(batch, h),
        in_specs=[
            pl.BlockSpec((seq, FOX_DH), lambda b, j: (b, q_off // FOX_DH + j)),
            pl.BlockSpec((seq, FOX_DH), lambda b, j: (b, j)),
            pl.BlockSpec((seq, FOX_DH), lambda b, j: (b, j)),
            pl.BlockSpec((None, 1, seq), lambda b, j: (b * h + j, 0, 0)),
        ],
        out_specs=pl.BlockSpec((seq, FOX_DH), lambda b, j: (b, j)),
        out_shape=jax.ShapeDtypeStruct((t, w), BF16),
        scratch_shapes=[pltpu.VMEM((seq, FOX_DH), BF16)] * 3,
        compiler_params=_params("parallel", "parallel"),
        name="fox_prompt",
    )(z, fk, fv, c_t)


def _fox_bias_kernel(pt_ref, lfn_ref, pool_ref, o_ref, buf_ref, sem):
    b = pl.program_id(0)
    npg, h, ps = buf_ref.shape

    def page_copy(p):
        return pltpu.make_async_copy(pool_ref.at[pt_ref[b, p]], buf_ref.at[p], sem)

    for p in range(npg):
        page_copy(p).start()
    for p in range(npg):
        page_copy(p).wait()

    x = buf_ref[...]
    after = (lax.broadcasted_iota(I32, (ps, ps), 0) >= lax.broadcasted_iota(I32, (ps, ps), 1)).astype(F32)
    suffix = jnp.dot(x.reshape(npg * h, ps), after, precision=HIGHEST,
                     preferred_element_type=F32).reshape(npg, h, ps)
    run = lfn_ref[...]
    for p in range(npg - 1, -1, -1):
        o_ref[p] = suffix[p] - x[p] + run
        run = run + suffix[p][:, 0:1]


def fox_sample_bias(page_table, lf_new, pool_lf_t):
    b, npg = page_table.shape
    _, h, ps = pool_lf_t.shape
    return pl.pallas_call(
        _fox_bias_kernel,
        grid_spec=pltpu.PrefetchScalarGridSpec(
            num_scalar_prefetch=1,
            grid=(b,),
            in_specs=[pl.BlockSpec((None, h, 1), lambda i, pt: (i, 0, 0)),
                      pl.BlockSpec(memory_space=pl.ANY)],
            out_specs=pl.BlockSpec((None, npg, h, ps), lambda i, pt: (i, 0, 0, 0)),
            scratch_shapes=[pltpu.VMEM((npg, h, ps), F32), pltpu.SemaphoreType.DMA(())],
        ),
        out_shape=jax.ShapeDtypeStruct((b, npg, h, ps), F32),
        compiler_params=_params("arbitrary"),
        name="fox_sample_bias",
    )(page_table, lf_new, pool_lf_t)


def _fox_sample_kernel(pt_ref, q_ref, kn_ref, vn_ref, bias_ref, k_ref, v_ref, o_ref, m_ref, l_ref, acc_ref):
    p = pl.program_id(1)
    h, w = acc_ref.shape
    dh = w // h
    own = lax.broadcasted_iota(I32, (h, w), 1) // dh == lax.broadcasted_iota(I32, (h, w), 0)
    q = q_ref[...] * dh ** -0.5

    @pl.when(p == 0)
    def _():
        m_ref[...] = jnp.broadcast_to(
            jnp.sum(jnp.where(own, q * kn_ref[...], 0.0), axis=1, keepdims=True), m_ref.shape)
        l_ref[...] = jnp.ones_like(l_ref)
        acc_ref[...] = jnp.where(own, vn_ref[...], 0.0)

    qd = jnp.where(own, q, 0.0).astype(BF16)
    logit = _nt_dot(qd, k_ref[...].astype(BF16)) + bias_ref[...]
    m = m_ref[:, 0:1]
    m_new = jnp.maximum(m, jnp.max(logit, axis=-1, keepdims=True))
    alpha = jnp.exp(m - m_new)
    pr = jnp.exp(logit - m_new)
    l_ref[...] = alpha * l_ref[...] + jnp.sum(pr, axis=-1, keepdims=True)
    m_ref[...] = jnp.broadcast_to(m_new, m_ref.shape)
    acc_ref[...] = alpha * acc_ref[...] + jnp.dot(pr.astype(BF16), v_ref[...].astype(BF16),
                                                  preferred_element_type=F32)

    @pl.when(p == pl.num_programs(1) - 1)
    def _():
        out = jnp.where(own, acc_ref[...] / l_ref[:, 0:1], 0.0)
        o_ref[...] = jnp.sum(out, axis=0, keepdims=True).astype(o_ref.dtype)


def fox_sample(page_table, q, k_new, v_new, bias, pool_k, pool_v):
    b, npg = page_table.shape
    _, ps, w = pool_k.shape
    h = w // FOX_DH
    row = pl.BlockSpec((None, 1, w), lambda i, p, pt: (i, 0, 0))
    page = pl.BlockSpec((None, ps, w), lambda i, p, pt: (pt[i, p], 0, 0))
    out = pl.pallas_call(
        _fox_sample_kernel,
        grid_spec=pltpu.PrefetchScalarGridSpec(
            num_scalar_prefetch=1,
            grid=(b, npg),
            in_specs=[row, row, row,
                      pl.BlockSpec((None, None, h, ps), lambda i, p, pt: (i, p, 0, 0)),
                      page, page],
            out_specs=row,
            scratch_shapes=[pltpu.VMEM((h, LANES), F32), pltpu.VMEM((h, LANES), F32), pltpu.VMEM((h, w), F32)],
        ),
        out_shape=jax.ShapeDtypeStruct((b, 1, w), BF16),
        compiler_params=_params("parallel", "arbitrary"),
        name="fox_sample",
    )(page_table, q.reshape(b, 1, w), k_new.reshape(b, 1, w), v_new.reshape(b, 1, w), bias, pool_k, pool_v)
    return out.reshape(b, w)


def _merge_kernel(og_ref, of_ref, wg_ref, wf_ref, mg_ref, mf_ref, o_ref):
    yg = jnp.dot(og_ref[...], wg_ref[...], preferred_element_type=F32)
    yf = jnp.dot(of_ref[...], wf_ref[...], preferred_element_type=F32)
    o_ref[...] = (jax.nn.sigmoid(mg_ref[...]) * yg + jax.nn.sigmoid(mf_ref[...]) * yf).astype(o_ref.dtype)


def merge_branches(og, of, w_o_gla, w_o_fox, z, mg_off, mf_off):
    t, k = og.shape
    n = w_o_gla.shape[1]
    tm, tn = min(512, t), min(1024, n)
    return pl.pallas_call(
        _merge_kernel,
        grid=(n // tn, t // tm),
        in_specs=[pl.BlockSpec((tm, k), lambda j, i: (i, 0)),
                  pl.BlockSpec((tm, k), lambda j, i: (i, 0)),
                  pl.BlockSpec((k, tn), lambda j, i: (0, j)),
                  pl.BlockSpec((k, tn), lambda j, i: (0, j)),
                  pl.BlockSpec((tm, tn), lambda j, i: (i, mg_off // tn + j)),
                  pl.BlockSpec((tm, tn), lambda j, i: (i, mf_off // tn + j))],
        out_specs=pl.BlockSpec((tm, tn), lambda j, i: (i, j)),
        out_shape=jax.ShapeDtypeStruct((t, n), BF16),
        compiler_params=_params("parallel", "parallel"),
        name="merge_branches",
    )(og, of, w_o_gla, w_o_fox, z, z)


def _route_kernel(mix_ref, w_ref, x_ref, g_ref, wr_ref, br_ref, h_ref, hn_ref, id_ref, wt_ref):
    h1 = x_ref[...] + jnp.dot(mix_ref[...], w_ref[...], preferred_element_type=F32)
    h_ref[...] = h1
    hn = _rms(h1, g_ref[...])
    hn_ref[...] = hn
    logits = jnp.dot(hn, wr_ref[...], precision=HIGHEST, preferred_element_type=F32) + br_ref[...]
    lane = lax.broadcasted_iota(I32, logits.shape, 1)
    big = jnp.int32(2 ** 30)

    def top(vals):
        best = jnp.max(vals, axis=-1, keepdims=True)
        return best, jnp.min(jnp.where(vals == best, lane, big), axis=-1, keepdims=True)

    grp = jnp.where(lane < N_GROUPS, logits, NEG_INF)
    g_max, g_idx = top(grp)
    g_w = 1.0 / jnp.sum(jnp.exp(grp - g_max), axis=-1, keepdims=True)
    lo = N_GROUPS + g_idx * EXPERTS_PER_GROUP
    mine = jnp.where(lane >= lo, jnp.where(lane < lo + EXPERTS_PER_GROUP, logits, NEG_INF), NEG_INF)
    e1, i1 = top(mine)
    e2, i2 = top(jnp.where(lane == i1, NEG_INF, mine))
    z = jnp.sum(jnp.exp(mine - e1), axis=-1, keepdims=True)
    p1 = 1.0 / z
    p2 = jnp.exp(e2 - e1) / z
    w1 = g_w * p1 / (p1 + p2)
    w2 = g_w * p2 / (p1 + p2)
    id_ref[...] = jnp.where(lane == 0, i1 - N_GROUPS, jnp.where(lane == 1, i2 - N_GROUPS, 0))
    wt_ref[...] = jnp.where(lane == 0, w1, jnp.where(lane == 1, w2, 0.0))


def out_proj_and_route(mix, w_out, x, g_ffn, w_router, b_router):
    t, d = x.shape
    tm = min(256, t)
    row = lambda n: pl.BlockSpec((tm, n), lambda i: (i, 0))
    full = lambda a: pl.BlockSpec(a.shape, lambda i: (0, 0))
    g2, b2 = g_ffn.reshape(1, d), b_router.reshape(1, LANES)
    return pl.pallas_call(
        _route_kernel,
        grid=(t // tm,),
        in_specs=[row(d), full(w_out), row(d), full(g2), full(w_router), full(b2)],
        out_specs=[row(d), row(d), row(LANES), row(LANES)],
        out_shape=[jax.ShapeDtypeStruct((t, d), F32), jax.ShapeDtypeStruct((t, d), F32),
                   jax.ShapeDtypeStruct((t, LANES), I32), jax.ShapeDtypeStruct((t, LANES), F32)],
        compiler_params=_params("parallel"),
        name="out_proj_and_route",
    )(mix, w_out, x, g2, w_router, b2)


def _rank_kernel(id_ref, rank_ref, cnt_ref):
    @pl.when(pl.program_id(0) == 0)
    def _():
        cnt_ref[...] = jnp.zeros_like(cnt_ref)

    ids = id_ref[...]
    n = ids.shape[0]
    lane = lax.broadcasted_iota(I32, ids.shape, 1)
    hit0 = lane == ids[:, 0:1]
    hit1 = lane == ids[:, 1:2]
    hits = (jnp.where(hit0, 1.0, 0.0) + jnp.where(hit1, 1.0, 0.0)).astype(BF16)
    lower = (lax.broadcasted_iota(I32, (n, n), 0) >= lax.broadcasted_iota(I32, (n, n), 1)).astype(BF16)
    seen = jnp.dot(lower, hits, preferred_element_type=F32) + cnt_ref[...]
    r0 = jnp.sum(jnp.where(hit0, seen, 0.0), axis=-1, keepdims=True) - 1.0
    r1 = jnp.sum(jnp.where(hit1, seen, 0.0), axis=-1, keepdims=True) - 1.0
    rank_ref[...] = jnp.where(lane == 0, r0, jnp.where(lane == 1, r1, 0.0)).astype(I32)
    cnt_ref[...] = seen[n - 1:n, :]


def rank_slots(ids):
    t = ids.shape[0]
    tm = 512
    return pl.pallas_call(
        _rank_kernel,
        grid=(t // tm,),
        in_specs=[pl.BlockSpec((tm, LANES), lambda i: (i, 0))],
        out_specs=[pl.BlockSpec((tm, LANES), lambda i: (i, 0)), pl.BlockSpec((1, LANES), lambda i: (0, 0))],
        out_shape=[jax.ShapeDtypeStruct((t, LANES), I32), jax.ShapeDtypeStruct((1, LANES), F32)],
        compiler_params=_params("arbitrary"),
        name="rank_slots",
    )(ids)


def _expert_kernel(be_ref, na_ref, tok_ref, x_ref, wg_ref, wu_ref, wd_ref, o_ref, buf_ref, sem):
    blk = pl.program_id(0)
    rows = buf_ref.shape[0]

    def row_copy(r):
        tok = tok_ref[blk * rows + r]
        return pltpu.make_async_copy(x_ref.at[pl.ds(tok, 1), :], buf_ref.at[pl.ds(r, 1), :], sem)

    @pl.when(blk < na_ref[0])
    def _():
        def start(r, c):
            row_copy(r).start()
            return c

        def wait(r, c):
            row_copy(r).wait()
            return c

        lax.fori_loop(0, rows, start, 0)
        lax.fori_loop(0, rows, wait, 0)
        x = buf_ref[...].astype(BF16)
        gate = jnp.dot(x, wg_ref[...], preferred_element_type=F32)
        up = jnp.dot(x, wu_ref[...], preferred_element_type=F32)
        act = (gate * jax.nn.sigmoid(gate) * up).astype(BF16)
        o_ref[...] = jnp.dot(act, wd_ref[...], preferred_element_type=F32)

    @pl.when(blk >= na_ref[0])
    def _():
        o_ref[...] = jnp.zeros_like(o_ref)


def run_experts(block_expert, n_active, row_token, x, w_gate, w_up, w_down):
    n_blocks = block_expert.shape[0]
    _, d, ff = w_gate.shape
    rows = MOE_ROWS
    wspec = lambda a, b: pl.BlockSpec((None, a, b), lambda i, be, na, tk: (be[i], 0, 0))
    return pl.pallas_call(
        _expert_kernel,
        grid_spec=pltpu.PrefetchScalarGridSpec(
            num_scalar_prefetch=3,
            grid=(n_blocks,),
            in_specs=[pl.BlockSpec(memory_space=pl.ANY), wspec(d, ff), wspec(d, ff), wspec(ff, d)],
            out_specs=pl.BlockSpec((rows, d), lambda i, be, na, tk: (i, 0)),
            scratch_shapes=[pltpu.VMEM((rows, d), F32), pltpu.SemaphoreType.DMA(())],
        ),
        out_shape=jax.ShapeDtypeStruct((n_blocks * rows, d), F32),
        compiler_params=_params("arbitrary"),
        name="run_experts",
    )(block_expert, n_active, row_token, x, w_gate, w_up, w_down)


def _combine_kernel(dest_ref, h_ref, wt_ref, p_ref, gp_ref, wpg_ref, wpp_ref, gf_ref, y_ref, o_ref, buf_ref, sem):
    i = pl.program_id(0)
    tm = h_ref.shape[0]

    def row_copy(r, c):
        src = dest_ref[(i * tm + r) * TOP_K + c]
        return pltpu.make_async_copy(y_ref.at[pl.ds(src, 1), :], buf_ref.at[c, pl.ds(r, 1), :], sem.at[c])

    def start(r, carry):
        for c in range(TOP_K):
            row_copy(r, c).start()
        return carry

    def wait(r, carry):
        for c in range(TOP_K):
            row_copy(r, c).wait()
        return carry

    lax.fori_loop(0, tm, start, 0)
    lax.fori_loop(0, tm, wait, 0)
    wt = wt_ref[...]
    h2 = h_ref[...] + (wt[:, 0:1] * buf_ref[0] + wt[:, 1:2] * buf_ref[1])
    gate = jax.nn.sigmoid(jnp.dot(_rms(h2, gp_ref[...]).astype(BF16), wpg_ref[...], preferred_element_type=F32))
    ple = jnp.dot(p_ref[...].astype(BF16), wpp_ref[...], preferred_element_type=F32)
    o_ref[...] = _rms(h2 + gate * ple, gf_ref[...])


def combine_and_finish(dest, h1, wts, p, g_ple, w_ple_gate, w_ple_proj, g_final, y_rows):
    t, d = h1.shape
    tm = min(256, t)
    row = lambda n: pl.BlockSpec((tm, n), lambda i, ds: (i, 0))
    full = lambda a: pl.BlockSpec(a.shape, lambda i, ds: (0, 0))
    gp, gf = g_ple.reshape(1, d), g_final.reshape(1, d)
    return pl.pallas_call(
        _combine_kernel,
        grid_spec=pltpu.PrefetchScalarGridSpec(
            num_scalar_prefetch=1,
            grid=(t // tm,),
            in_specs=[row(d), row(LANES), row(p.shape[1]), full(gp), full(w_ple_gate), full(w_ple_proj), full(gf),
                      pl.BlockSpec(memory_space=pl.ANY)],
            out_specs=row(d),
            scratch_shapes=[pltpu.VMEM((TOP_K, tm, d), F32), pltpu.SemaphoreType.DMA((TOP_K,))],
        ),
        out_shape=jax.ShapeDtypeStruct((t, d), F32),
        compiler_params=_params("arbitrary"),
        name="combine_and_finish",
    )(dest, h1, wts, p, gp, w_ple_gate, w_ple_proj, gf, y_rows)


def _segments(d):
    qk = d // 2
    sizes = dict(gq=qk, gk=qk, gv=d, gr=d, ga=GLA_RANK, fq=d, fk=d, fv=d, ff=d // FOX_DH, mg=d, mf=d)
    start, acc = {}, 0
    for name, size in sizes.items():
        start[name] = acc
        acc += size
    return sizes, start


def _project(x, g_mix, w_main, w_k, w_v, w_small, w_a2_pad, b_a):
    u = rms_cast(x, g_mix)
    return (matmul(u, w_main), matmul(u, w_k), matmul(u, w_v),
            *_small(matmul(u, w_small, tn=LANES), w_a2_pad, b_a))


def _small(z_small, w_a2_pad, b_a):
    return z_small, gla_log_decay(z_small, w_a2_pad, b_a)


def kernel(x_prompt, x_sample, cache_fox_k, cache_fox_v, cache_fox_logf, state_gla, page_table,
           p_prompt, p_sample, g_mix, w_in, w_gla_a2, b_gla_a, b_fox_f, g_gla_out, w_o_gla, w_o_fox, w_out,
           g_ffn, w_route_group, b_route_group, w_route_expert, b_route_expert, w_exp_gate, w_exp_up,
           w_exp_down, g_ple, w_ple_gate, w_ple_proj, g_final):
    depth = w_in.shape[0]
    bp, sp, d = x_prompt.shape
    bs, ss, _ = x_sample.shape
    assert ss == 1, "the sample group carries one new token per sequence"
    fox_heads = d // FOX_DH
    n_phys, page_size = cache_fox_k.shape[1:3]
    tp, ts = bp * sp, bs * ss
    sizes, start = _segments(d)
    main = ("gq", "gk", "gv", "gr", "fq", "mg", "mf")
    off, acc = {}, 0
    for name in main:
        off[name] = acc
        acc += sizes[name]

    hp = x_prompt.reshape(tp, d)
    hs = x_sample.reshape(ts, d)
    outs = {name: [] for name in ("kp", "vp", "lfp", "sp", "ks", "vs", "lfs", "ss")}
    for i in range(depth):
        w = w_in[i]
        seg = lambda name: w[:, start[name]:start[name] + sizes[name]]
        w_main = jnp.concatenate([seg(n) for n in main], axis=1).astype(BF16)
        w_k, w_v = seg("fk").astype(BF16), seg("fv").astype(BF16)
        w_small = jnp.pad(jnp.concatenate([seg("ga"), seg("ff")], axis=1),
                          ((0, 0), (0, LANES - GLA_RANK - fox_heads))).astype(BF16)
        w_a2_pad = jnp.pad(w_gla_a2[i], ((0, LANES - GLA_RANK), (0, 0)))
        w_router = jnp.pad(jnp.concatenate([w_route_group[i], w_route_expert[i]], axis=1),
                           ((0, 0), (0, LANES - N_GROUPS - N_EXPERTS)))
        b_router = jnp.pad(jnp.concatenate([b_route_group[i], b_route_expert[i]]),
                           (0, LANES - N_GROUPS - N_EXPERTS))
        wog, wof, wo = w_o_gla[i].astype(BF16), w_o_fox[i].astype(BF16), w_out[i].astype(BF16)
        proj = (g_mix[i], w_main, w_k, w_v, w_small, w_a2_pad, b_gla_a[i])

        def forget(z_small, batch, seq):
            f_t = z_small[:, GLA_RANK:GLA_RANK + fox_heads].reshape(batch, seq, fox_heads).transpose(0, 2, 1)
            return fox_log_forget(f_t, b_fox_f[i])

        zp, fkp, fvp, zsp, lap = _project(hp, *proj)
        lf_t, c_t = forget(zsp, bp, sp)
        ogp, s_fin = gla_prompt(zp, lap, g_gla_out[i], bp, sp, off)
        ofp = fox_prompt(zp, fkp, fvp, c_t.reshape(bp * fox_heads, 1, sp), bp, sp, off["fq"])
        mixp = merge_branches(ogp, ofp, wog, wof, zp, off["mg"], off["mf"])
        h1p, hnp, idp, wtp = out_proj_and_route(mixp, wo, hp, g_ffn[i], w_router, b_router)
        outs["kp"].append(fkp.reshape(bp, sp, fox_heads, FOX_DH))
        outs["vp"].append(fvp.reshape(bp, sp, fox_heads, FOX_DH))
        outs["lfp"].append(lf_t.transpose(0, 2, 1))
        outs["sp"].append(s_fin)

        zs, fks, fvs, zss, las = _project(hs, *proj)
        lfs_t, _ = forget(zss, 1, ts)
        lfs = lfs_t[0].T
        ogs, s_new = gla_sample(zs, las, g_gla_out[i], state_gla[i], off)
        bias = fox_sample_bias(page_table, lfs.reshape(bs, fox_heads, 1),
                               cache_fox_logf[i].transpose(0, 2, 1))
        ofs = fox_sample(page_table, zs[:, off["fq"]:off["fq"] + d], fks, fvs, bias,
                         cache_fox_k[i].reshape(n_phys, page_size, d), cache_fox_v[i].reshape(n_phys, page_size, d))
        mixs = merge_branches(ogs, ofs, wog, wof, zs, off["mg"], off["mf"])
        h1s, hns, ids, wts = out_proj_and_route(mixs, wo, hs, g_ffn[i], w_router, b_router)
        outs["ks"].append(fks.reshape(bs, ss, fox_heads, FOX_DH))
        outs["vs"].append(fvs.reshape(bs, ss, fox_heads, FOX_DH))
        outs["lfs"].append(lfs.reshape(bs, ss, fox_heads))
        outs["ss"].append(s_new)

        t_all = tp + ts
        n_slots = t_all * TOP_K
        t_pad = -(-t_all // 512) * 512
        ids_all = jnp.concatenate([idp, ids], axis=0)
        ids_pad = jnp.pad(ids_all, ((0, t_pad - t_all), (0, 0)), constant_values=LANES - 1)
        rank, counts = rank_slots(ids_pad)
        counts = counts[0, :N_EXPERTS].astype(I32)
        padded = -(-counts // MOE_ROWS) * MOE_ROWS
        ends = jnp.cumsum(padded)
        experts = ids_all[:, :TOP_K]
        dest = (ends - padded)[experts] + rank[:t_all, :TOP_K]
        n_blocks = -(-n_slots // MOE_ROWS) + N_EXPERTS
        n_active = (ends[-1] // MOE_ROWS).reshape(1)
        block_start = jnp.arange(n_blocks, dtype=I32) * MOE_ROWS
        block_expert = jnp.searchsorted(ends, jnp.minimum(block_start, ends[-1] - 1), side="right").astype(I32)
        tokens = jnp.broadcast_to(jnp.arange(t_all, dtype=I32)[:, None], (t_all, TOP_K))
        row_token = jnp.zeros((n_blocks * MOE_ROWS,), I32).at[dest.reshape(-1)].set(tokens.reshape(-1))
        y_rows = run_experts(block_expert, n_active.astype(I32), row_token, jnp.concatenate([hnp, hns], axis=0),
                             w_exp_gate[i].astype(BF16), w_exp_up[i].astype(BF16), w_exp_down[i].astype(BF16))

        last = i == depth - 1
        assert last, "the final norm is fused into the last layer's finishing kernel"
        fin = (g_ple[i], w_ple_gate[i].astype(BF16), w_ple_proj[i].astype(BF16), g_final, y_rows)
        hp = combine_and_finish(dest[:tp].reshape(-1), h1p, wtp, p_prompt[i].reshape(tp, -1), *fin)
        hs = combine_and_finish(dest[tp:].reshape(-1), h1s, wts, p_sample[i].reshape(ts, -1), *fin)

    stack = lambda name: jnp.stack(outs[name])
    return (hp.reshape(bp, sp, d), hs.reshape(bs, ss, d),
            stack("kp"), stack("vp"), stack("lfp"), stack("sp"),
            stack("ks"), stack("vs"), stack("lfs"), stack("ss"))
```

```python
import functools

import jax
import jax.numpy as jnp
from jax import lax
from jax.experimental import pallas as pl
from jax.experimental.pallas import tpu as pltpu

F32 = jnp.float32
BF16 = jnp.bfloat16
I32 = jnp.int32
HIGHEST = lax.Precision.HIGHEST
NEG_INF = float("-inf")

RMS_EPS = 1e-6
GLA_HEADS = 4
GLA_RANK = 16
GLA_TAU = 16.0
GLA_CHUNK = 64
GLA_SUB = 16
FOX_DH = 128
N_GROUPS = 4
EXPERTS_PER_GROUP = 8
N_EXPERTS = N_GROUPS * EXPERTS_PER_GROUP
TOP_K = 2

LANES = 128
MOE_ROWS = 256
GATHER_UNROLL = 8
VMEM_LIMIT = 56 * 2**20


def _params(*sem):
    return pltpu.CompilerParams(dimension_semantics=sem, vmem_limit_bytes=VMEM_LIMIT)


def _log_sigmoid(x):
    return jnp.minimum(x, 0.0) - jnp.log1p(jnp.exp(-jnp.abs(x)))


def _rms(x, g):
    return x * lax.rsqrt(jnp.mean(x * x, axis=-1, keepdims=True) + RMS_EPS) * g


def _nt_dot(a, b):
    return lax.dot_general(a, b, (((1,), (1,)), ((), ())), preferred_element_type=F32)


def _tn_dot(a, b):
    return lax.dot_general(a, b, (((0,), (0,)), ((), ())), preferred_element_type=F32)


def _row_to_col(row):
    n = row.shape[1]
    eye = lax.broadcasted_iota(I32, (n, n), 0) == lax.broadcasted_iota(I32, (n, n), 1)
    return jnp.sum(jnp.where(eye, row, 0.0), axis=1, keepdims=True)


def _rms_cast_kernel(x_ref, g_ref, o_ref):
    o_ref[...] = _rms(x_ref[...], g_ref[...]).astype(o_ref.dtype)


def rms_cast(x, g):
    t, d = x.shape
    tm = min(512, t)
    return pl.pallas_call(
        _rms_cast_kernel,
        grid=(t // tm,),
        in_specs=[pl.BlockSpec((tm, d), lambda i: (i, 0)), pl.BlockSpec((1, d), lambda i: (0, 0))],
        out_specs=pl.BlockSpec((tm, d), lambda i: (i, 0)),
        out_shape=jax.ShapeDtypeStruct((t, d), BF16),
        compiler_params=_params("parallel"),
        name="rms_cast",
    )(x, g.reshape(1, d))


def _mm_kernel(a_ref, w_ref, o_ref):
    o_ref[...] = jnp.dot(a_ref[...], w_ref[...], preferred_element_type=F32).astype(o_ref.dtype)


def matmul(a, w, out_dtype=F32, tm=512, tn=1024):
    t, k = a.shape
    n = w.shape[1]
    tm, tn = min(tm, t), min(tn, n)
    return pl.pallas_call(
        _mm_kernel,
        grid=(n // tn, t // tm),
        in_specs=[pl.BlockSpec((tm, k), lambda j, i: (i, 0)), pl.BlockSpec((k, tn), lambda j, i: (0, j))],
        out_specs=pl.BlockSpec((tm, tn), lambda j, i: (i, j)),
        out_shape=jax.ShapeDtypeStruct((t, n), out_dtype),
        compiler_params=_params("parallel", "parallel"),
        name="matmul",
    )(a, w)


def _la_kernel(z_ref, w_ref, b_ref, o_ref):
    pre = jnp.dot(z_ref[...].astype(BF16), w_ref[...], preferred_element_type=F32) + b_ref[...]
    o_ref[...] = _log_sigmoid(pre) / GLA_TAU


def gla_log_decay(z_small, w_a2_pad, b_a):
    t = z_small.shape[0]
    n = w_a2_pad.shape[1]
    tm = min(512, t)
    return pl.pallas_call(
        _la_kernel,
        grid=(t // tm,),
        in_specs=[pl.BlockSpec((tm, LANES), lambda i: (i, 0)),
                  pl.BlockSpec((LANES, n), lambda i: (0, 0)),
                  pl.BlockSpec((1, n), lambda i: (0, 0))],
        out_specs=pl.BlockSpec((tm, n), lambda i: (i, 0)),
        out_shape=jax.ShapeDtypeStruct((t, n), F32),
        compiler_params=_params("parallel"),
        name="gla_log_decay",
    )(z_small, w_a2_pad, b_a.reshape(1, n))


def _logf_kernel(f_ref, b_ref, lf_ref, c_ref, *, blk):
    lf = _log_sigmoid(f_ref[...] + b_ref[...])
    lf_ref[...] = lf
    s = lf.shape[1]
    upper = (lax.broadcasted_iota(I32, (blk, blk), 0) <= lax.broadcasted_iota(I32, (blk, blk), 1)).astype(F32)
    carry = jnp.zeros((lf.shape[0], 1), F32)
    for j in range(s // blk):
        cs = jnp.dot(lf[:, j * blk:(j + 1) * blk], upper, precision=HIGHEST, preferred_element_type=F32) + carry
        c_ref[:, j * blk:(j + 1) * blk] = cs
        carry = cs[:, blk - 1:blk]


def fox_log_forget(f_t, b_f):
    b, h, s = f_t.shape
    blk = min(256, s)
    spec = pl.BlockSpec((None, h, s), lambda i: (i, 0, 0))
    return pl.pallas_call(
        functools.partial(_logf_kernel, blk=blk),
        grid=(b,),
        in_specs=[spec, pl.BlockSpec((h, 1), lambda i: (0, 0))],
        out_specs=[spec, spec],
        out_shape=[jax.ShapeDtypeStruct((b, h, s), F32)] * 2,
        compiler_params=_params("parallel"),
        name="fox_log_forget",
    )(f_t, b_f.reshape(h, 1))


def _gla_prompt_kernel(q_ref, k_ref, v_ref, r_ref, la_ref, g_ref, o_ref, s_ref, *, rows):
    @pl.when(pl.program_id(2) == 0)
    def _():
        s_ref[...] = jnp.zeros_like(s_ref)

    c, sub = GLA_CHUNK, GLA_SUB
    nsub = c // sub
    dk = q_ref.shape[1]
    scale = dk ** -0.5
    row = lax.broadcasted_iota(I32, (c, c), 0)
    col = lax.broadcasted_iota(I32, (c, c), 1)
    lower = (row >= col).astype(F32)
    col_sub = lax.broadcasted_iota(I32, (sub, c), 1)
    row_in_sub = lax.broadcasted_iota(I32, (c, 1), 0) % sub

    def sub_rows(x, s):
        return jnp.concatenate(
            [jnp.broadcast_to(x[i * sub + s:i * sub + s + 1, :], (sub, dk)) for i in range(nsub)], axis=0)

    def chunk(ci, carry):
        sl = pl.ds(pl.multiple_of(ci * c, c), c)
        q = q_ref[sl, :] * scale
        k = k_ref[sl, :]
        vb = v_ref[sl, :].astype(BF16)
        b = jnp.dot(lower, la_ref[sl, :], precision=HIGHEST, preferred_element_type=F32)
        state = s_ref[...]

        o = jnp.dot((q * jnp.exp(b)).astype(BF16), state.astype(BF16), preferred_element_type=F32)

        a = jnp.zeros((c, c), F32)
        for s in range(sub):
            e = jnp.exp(jnp.where(row_in_sub >= s, b - sub_rows(b, s), NEG_INF))
            w = jnp.sum(q * e * sub_rows(k, s), axis=-1, keepdims=True)
            a = jnp.where(col == (row // sub) * sub + s, w, a)
        blocks = [a[:sub]]
        for i in range(1, nsub):
            ri = slice(i * sub, (i + 1) * sub)
            edge = b[i * sub - 1:i * sub, :]
            qs = (q[ri] * jnp.exp(b[ri] - edge)).astype(BF16)
            ks = (k * jnp.exp(jnp.minimum(edge - b, 0.0))).astype(BF16)
            blocks.append(jnp.where(col_sub < i * sub, _nt_dot(qs, ks), a[ri]))
        a = jnp.concatenate(blocks, axis=0)
        o = o + jnp.dot(a.astype(BF16), vb, preferred_element_type=F32)

        r = r_ref[sl, :]
        o_ref[sl, :] = (_rms(o, g_ref[...]) * (r * jax.nn.sigmoid(r))).astype(o_ref.dtype)

        last = b[c - 1:c, :]
        kd = (k * jnp.exp(last - b)).astype(BF16)
        s_ref[...] = _row_to_col(jnp.exp(last)) * state + _tn_dot(kd, vb)
        return carry

    lax.fori_loop(0, rows // c, chunk, 0)


def gla_prompt(z, la, g_out, batch, seq, off):
    t = z.shape[0]
    dk = la.shape[1] // GLA_HEADS
    dv = g_out.shape[0]
    rows = min(512, seq)
    nt = seq // rows
    kern = functools.partial(_gla_prompt_kernel, rows=rows)
    rmap = lambda b, h, i: b * nt + i
    return pl.pallas_call(
        kern,
        grid=(batch, GLA_HEADS, nt),
        in_specs=[
            pl.BlockSpec((rows, dk), lambda b, h, i: (rmap(b, h, i), off["gq"] // dk + h)),
            pl.BlockSpec((rows, dk), lambda b, h, i: (rmap(b, h, i), off["gk"] // dk + h)),
            pl.BlockSpec((rows, dv), lambda b, h, i: (rmap(b, h, i), off["gv"] // dv + h)),
            pl.BlockSpec((rows, dv), lambda b, h, i: (rmap(b, h, i), off["gr"] // dv + h)),
            pl.BlockSpec((rows, dk), lambda b, h, i: (rmap(b, h, i), h)),
            pl.BlockSpec((1, dv), lambda b, h, i: (0, 0)),
        ],
        out_specs=[
            pl.BlockSpec((rows, dv), lambda b, h, i: (rmap(b, h, i), h)),
            pl.BlockSpec((None, None, dk, dv), lambda b, h, i: (b, h, 0, 0)),
        ],
        out_shape=[jax.ShapeDtypeStruct((t, GLA_HEADS * dv), BF16),
                   jax.ShapeDtypeStruct((batch, GLA_HEADS, dk, dv), F32)],
        compiler_params=_params("parallel", "parallel", "arbitrary"),
        name="gla_prompt",
    )(z, z, z, z, la, g_out.reshape(1, dv))


def _gla_sample_kernel(q_ref, k_ref, v_ref, r_ref, la_ref, g_ref, s_ref, o_ref, sn_ref):
    dk = q_ref.shape[1]
    q = q_ref[...] * dk ** -0.5
    k = k_ref[...]
    v = v_ref[...]
    decay = jnp.exp(la_ref[...])
    state = s_ref[...]
    sn_ref[...] = _row_to_col(decay) * state + _row_to_col(k) * v
    qe = _row_to_col((q * decay).astype(BF16).astype(F32))
    o = jnp.sum(qe * state.astype(BF16).astype(F32), axis=0, keepdims=True)
    o = o + jnp.sum(q * k, axis=1, keepdims=True) * v
    r = r_ref[...]
    o_ref[...] = (_rms(o, g_ref[...]) * (r * jax.nn.sigmoid(r))).astype(o_ref.dtype)


def gla_sample(z, la, g_out, state, off):
    b = z.shape[0]
    _, h, dk, dv = state.shape
    z3 = z.reshape(b, 1, z.shape[1])
    la3 = la.reshape(b, 1, la.shape[1])
    og, sn = pl.pallas_call(
        _gla_sample_kernel,
        grid=(b, h),
        in_specs=[
            pl.BlockSpec((None, 1, dk), lambda i, j: (i, 0, off["gq"] // dk + j)),
            pl.BlockSpec((None, 1, dk), lambda i, j: (i, 0, off["gk"] // dk + j)),
            pl.BlockSpec((None, 1, dv), lambda i, j: (i, 0, off["gv"] // dv + j)),
            pl.BlockSpec((None, 1, dv), lambda i, j: (i, 0, off["gr"] // dv + j)),
            pl.BlockSpec((None, 1, dk), lambda i, j: (i, 0, j)),
            pl.BlockSpec((1, dv), lambda i, j: (0, 0)),
            pl.BlockSpec((None, None, dk, dv), lambda i, j: (i, j, 0, 0)),
        ],
        out_specs=[
            pl.BlockSpec((None, 1, dv), lambda i, j: (i, 0, j)),
            pl.BlockSpec((None, None, dk, dv), lambda i, j: (i, j, 0, 0)),
        ],
        out_shape=[jax.ShapeDtypeStruct((b, 1, h * dv), BF16), jax.ShapeDtypeStruct(state.shape, F32)],
        compiler_params=_params("parallel", "parallel"),
        name="gla_sample",
    )(z3, z3, z3, z3, la3, g_out.reshape(1, dv), state)
    return og.reshape(b, h * dv), sn


def _fox_prompt_kernel(q_ref, k_ref, v_ref, c_ref, o_ref, kb_ref, vb_ref, *, blk):
    s, dh = q_ref.shape
    kb_ref[...] = k_ref[...].astype(BF16)
    vb_ref[...] = v_ref[...].astype(BF16)
    causal = lax.broadcasted_iota(I32, (blk, blk), 0) >= lax.broadcasted_iota(I32, (blk, blk), 1)
    for i in range(s // blk):
        qs = slice(i * blk, (i + 1) * blk)
        q = (q_ref[qs, :] * dh ** -0.5).astype(BF16)
        m = jnp.full((blk, 1), NEG_INF, F32)
        l = jnp.zeros((blk, 1), F32)
        acc = jnp.zeros((blk, dh), F32)
        for j in range(i + 1):
            ks = slice(j * blk, (j + 1) * blk)
            logit = _nt_dot(q, kb_ref[ks, :]) - c_ref[:, ks]
            if j == i:
                logit = jnp.where(causal, logit, NEG_INF)
            m_new = jnp.maximum(m, jnp.max(logit, axis=-1, keepdims=True))
            alpha = jnp.exp(m - m_new)
            p = jnp.exp(logit - m_new)
            l = alpha * l + jnp.sum(p, axis=-1, keepdims=True)
            acc = alpha * acc + jnp.dot(p.astype(BF16), vb_ref[ks, :], preferred_element_type=F32)
            m = m_new
        o_ref[qs, :] = (acc / l).astype(o_ref.dtype)


def fox_prompt(z, fk, fv, c_t, batch, seq, q_off):
    t, w = fk.shape
    h = w // FOX_DH
    blk = min(512, seq)
    return pl.pallas_call(
        functools.partial(_fox_prompt_kernel, blk=blk),
        grid=(batch, h),
        in_specs=[
            pl.BlockSpec((seq, FOX_DH), lambda b, j: (b, q_off // FOX_DH + j)),
            pl.BlockSpec((seq, FOX_DH), lambda b, j: (b, j)),
            pl.BlockSpec((seq, FOX_DH), lambda b, j: (b, j)),
            pl.BlockSpec((None, 1, seq), lambda b, j: (b * h + j, 0, 0)),
        ],
        out_specs=pl.BlockSpec((seq, FOX_DH), lambda b, j: (b, j)),
        out_shape=jax.ShapeDtypeStruct((t, w), BF16),
        scratch_shapes=[pltpu.VMEM((seq, FOX_DH), BF16)] * 2,
        compiler_params=_params("parallel", "parallel"),
        name="fox_prompt",
    )(z, fk, fv, c_t)


def _fox_bias_kernel(pt_ref, lfn_ref, pool_ref, o_ref, buf_ref, sem):
    b = pl.program_id(0)
    npg, h, ps = buf_ref.shape

    def page_copy(p):
        return pltpu.make_async_copy(pool_ref.at[pt_ref[b, p]], buf_ref.at[p], sem)

    for p in range(npg):
        page_copy(p).start()
    for p in range(npg):
        page_copy(p).wait()

    x = buf_ref[...]
    after = (lax.broadcasted_iota(I32, (ps, ps), 0) >= lax.broadcasted_iota(I32, (ps, ps), 1)).astype(F32)
    suffix = jnp.dot(x.reshape(npg * h, ps), after, precision=HIGHEST,
                     preferred_element_type=F32).reshape(npg, h, ps)
    run = lfn_ref[...]
    for p in range(npg - 1, -1, -1):
        o_ref[p] = suffix[p] - x[p] + run
        run = run + suffix[p][:, 0:1]


def fox_sample_bias(page_table, lf_new, pool_lf_t):
    b, npg = page_table.shape
    _, h, ps = pool_lf_t.shape
    return pl.pallas_call(
        _fox_bias_kernel,
        grid_spec=pltpu.PrefetchScalarGridSpec(
            num_scalar_prefetch=1,
            grid=(b,),
            in_specs=[pl.BlockSpec((None, h, 1), lambda i, pt: (i, 0, 0)),
                      pl.BlockSpec(memory_space=pl.ANY)],
            out_specs=pl.BlockSpec((None, npg, h, ps), lambda i, pt: (i, 0, 0, 0)),
            scratch_shapes=[pltpu.VMEM((npg, h, ps), F32), pltpu.SemaphoreType.DMA(())],
        ),
        out_shape=jax.ShapeDtypeStruct((b, npg, h, ps), F32),
        compiler_params=_params("arbitrary"),
        name="fox_sample_bias",
    )(page_table, lf_new, pool_lf_t)


def _fox_sample_kernel(pt_ref, q_ref, kn_ref, vn_ref, bias_ref, *refs, pages):
    k_refs, v_refs = refs[:pages], refs[pages:2 * pages]
    o_ref, m_ref, l_ref, acc_ref = refs[2 * pages:]
    p = pl.program_id(1)
    h, dh = q_ref.shape
    n = k_refs[0].shape[0] * h
    q = q_ref[...] * dh ** -0.5

    @pl.when(p == 0)
    def _():
        m_ref[...] = jnp.broadcast_to(jnp.sum(q * kn_ref[...], axis=1, keepdims=True), m_ref.shape)
        l_ref[...] = jnp.ones_like(l_ref)
        acc_ref[...] = vn_ref[...]

    own = lax.broadcasted_iota(I32, (h, n), 1) % h == lax.broadcasted_iota(I32, (h, n), 0)
    qb = q.astype(BF16)
    logits = [jnp.where(own, _nt_dot(qb, k_refs[r][...].reshape(n, dh).astype(BF16)) + bias_ref[r:r + 1, :], NEG_INF)
              for r in range(pages)]
    m = m_ref[:, 0:1]
    m_new = m
    for lg in logits:
        m_new = jnp.maximum(m_new, jnp.max(lg, axis=-1, keepdims=True))
    alpha = jnp.exp(m - m_new)
    l = alpha * l_ref[:, 0:1]
    acc = alpha * acc_ref[...]
    for r in range(pages):
        pr = jnp.exp(logits[r] - m_new)
        l = l + jnp.sum(pr, axis=-1, keepdims=True)
        acc = acc + jnp.dot(pr.astype(BF16), v_refs[r][...].reshape(n, dh).astype(BF16), preferred_element_type=F32)
    m_ref[...] = jnp.broadcast_to(m_new, m_ref.shape)
    l_ref[...] = jnp.broadcast_to(l, l_ref.shape)
    acc_ref[...] = acc

    @pl.when(p == pl.num_programs(1) - 1)
    def _():
        o_ref[...] = (acc / l).astype(o_ref.dtype)


FOX_PAGES_PER_STEP = 4


def fox_sample(page_table, q, k_new, v_new, bias, cache_k, cache_v, layer):
    b, npg = page_table.shape
    _, _, ps, h, dh = cache_k.shape
    g = FOX_PAGES_PER_STEP
    row = pl.BlockSpec((None, h, dh), lambda i, p, pt: (i, 0, 0))

    def page(r):
        return pl.BlockSpec((None, None, ps, h, dh), lambda i, p, pt: (layer, pt[i, p * g + r], 0, 0, 0))

    pages = [page(r) for r in range(g)]
    return pl.pallas_call(
        functools.partial(_fox_sample_kernel, pages=g),
        grid_spec=pltpu.PrefetchScalarGridSpec(
            num_scalar_prefetch=1,
            grid=(b, npg // g),
            in_specs=[row, row, row,
                      pl.BlockSpec((None, None, g, ps * h), lambda i, p, pt: (i, p, 0, 0)),
                      *pages, *pages],
            out_specs=row,
            scratch_shapes=[pltpu.VMEM((h, LANES), F32), pltpu.VMEM((h, LANES), F32), pltpu.VMEM((h, dh), F32)],
        ),
        out_shape=jax.ShapeDtypeStruct((b, h, dh), BF16),
        compiler_params=_params("parallel", "arbitrary"),
        name="fox_sample",
    )(page_table, q, k_new, v_new, bias, *([cache_k] * g), *([cache_v] * g))


def _merge_kernel(og_ref, of_ref, wg_ref, wf_ref, mg_ref, mf_ref, o_ref):
    yg = jnp.dot(og_ref[...], wg_ref[...], preferred_element_type=F32)
    yf = jnp.dot(of_ref[...], wf_ref[...], preferred_element_type=F32)
    o_ref[...] = (jax.nn.sigmoid(mg_ref[...]) * yg + jax.nn.sigmoid(mf_ref[...]) * yf).astype(o_ref.dtype)


def merge_branches(og, of, w_o_gla, w_o_fox, z, mg_off, mf_off):
    t, k = og.shape
    n = w_o_gla.shape[1]
    tm, tn = min(512, t), min(1024, n)
    return pl.pallas_call(
        _merge_kernel,
        grid=(n // tn, t // tm),
        in_specs=[pl.BlockSpec((tm, k), lambda j, i: (i, 0)),
                  pl.BlockSpec((tm, k), lambda j, i: (i, 0)),
                  pl.BlockSpec((k, tn), lambda j, i: (0, j)),
                  pl.BlockSpec((k, tn), lambda j, i: (0, j)),
                  pl.BlockSpec((tm, tn), lambda j, i: (i, mg_off // tn + j)),
                  pl.BlockSpec((tm, tn), lambda j, i: (i, mf_off // tn + j))],
        out_specs=pl.BlockSpec((tm, tn), lambda j, i: (i, j)),
        out_shape=jax.ShapeDtypeStruct((t, n), BF16),
        compiler_params=_params("parallel", "parallel"),
        name="merge_branches",
    )(og, of, w_o_gla, w_o_fox, z, z)


def _route_kernel(mix_ref, w_ref, x_ref, g_ref, wr_ref, br_ref, h_ref, hn_ref, id_ref, wt_ref):
    h1 = x_ref[...] + jnp.dot(mix_ref[...], w_ref[...], preferred_element_type=F32)
    h_ref[...] = h1
    hn = _rms(h1, g_ref[...])
    hn_ref[...] = hn
    logits = jnp.dot(hn.astype(BF16), wr_ref[...], preferred_element_type=F32) + br_ref[...]
    lane = lax.broadcasted_iota(I32, logits.shape, 1)
    big = jnp.int32(2 ** 30)

    def top(vals):
        best = jnp.max(vals, axis=-1, keepdims=True)
        return best, jnp.min(jnp.where(vals == best, lane, big), axis=-1, keepdims=True)

    grp = jnp.where(lane < N_GROUPS, logits, NEG_INF)
    g_max, g_idx = top(grp)
    g_w = 1.0 / jnp.sum(jnp.exp(grp - g_max), axis=-1, keepdims=True)
    lo = N_GROUPS + g_idx * EXPERTS_PER_GROUP
    mine = jnp.where(lane >= lo, jnp.where(lane < lo + EXPERTS_PER_GROUP, logits, NEG_INF), NEG_INF)
    e1, i1 = top(mine)
    e2, i2 = top(jnp.where(lane == i1, NEG_INF, mine))
    z = jnp.sum(jnp.exp(mine - e1), axis=-1, keepdims=True)
    p1 = 1.0 / z
    p2 = jnp.exp(e2 - e1) / z
    w1 = g_w * p1 / (p1 + p2)
    w2 = g_w * p2 / (p1 + p2)
    id_ref[...] = jnp.where(lane == 0, i1 - N_GROUPS, jnp.where(lane == 1, i2 - N_GROUPS, 0))
    wt_ref[...] = jnp.where(lane == 0, w1, jnp.where(lane == 1, w2, 0.0))


def out_proj_and_route(mix, w_out, x, g_ffn, w_router, b_router):
    t, d = x.shape
    tm = min(256, t)
    row = lambda n: pl.BlockSpec((tm, n), lambda i: (i, 0))
    full = lambda a: pl.BlockSpec(a.shape, lambda i: (0, 0))
    g2, b2 = g_ffn.reshape(1, d), b_router.reshape(1, LANES)
    return pl.pallas_call(
        _route_kernel,
        grid=(t // tm,),
        in_specs=[row(d), full(w_out), row(d), full(g2), full(w_router), full(b2)],
        out_specs=[row(d), row(d), row(LANES), row(LANES)],
        out_shape=[jax.ShapeDtypeStruct((t, d), F32), jax.ShapeDtypeStruct((t, d), F32),
                   jax.ShapeDtypeStruct((t, LANES), I32), jax.ShapeDtypeStruct((t, LANES), F32)],
        compiler_params=_params("parallel"),
        name="out_proj_and_route",
    )(mix, w_out, x, g2, w_router, b2)


def _rank_kernel(id_ref, rank_ref, cnt_ref):
    @pl.when(pl.program_id(0) == 0)
    def _():
        cnt_ref[...] = jnp.zeros_like(cnt_ref)

    ids = id_ref[...]
    n = ids.shape[0]
    lane = lax.broadcasted_iota(I32, ids.shape, 1)
    hit0 = lane == ids[:, 0:1]
    hit1 = lane == ids[:, 1:2]
    hits = (jnp.where(hit0, 1.0, 0.0) + jnp.where(hit1, 1.0, 0.0)).astype(BF16)
    lower = (lax.broadcasted_iota(I32, (n, n), 0) >= lax.broadcasted_iota(I32, (n, n), 1)).astype(BF16)
    seen = jnp.dot(lower, hits, preferred_element_type=F32) + cnt_ref[...]
    r0 = jnp.sum(jnp.where(hit0, seen, 0.0), axis=-1, keepdims=True) - 1.0
    r1 = jnp.sum(jnp.where(hit1, seen, 0.0), axis=-1, keepdims=True) - 1.0
    rank_ref[...] = jnp.where(lane == 0, r0, jnp.where(lane == 1, r1, 0.0)).astype(I32)
    cnt_ref[...] = seen[n - 1:n, :]


def rank_slots(ids):
    t = ids.shape[0]
    tm = 512
    return pl.pallas_call(
        _rank_kernel,
        grid=(t // tm,),
        in_specs=[pl.BlockSpec((tm, LANES), lambda i: (i, 0))],
        out_specs=[pl.BlockSpec((tm, LANES), lambda i: (i, 0)), pl.BlockSpec((1, LANES), lambda i: (0, 0))],
        out_shape=[jax.ShapeDtypeStruct((t, LANES), I32), jax.ShapeDtypeStruct((1, LANES), F32)],
        compiler_params=_params("arbitrary"),
        name="rank_slots",
    )(ids)


def _expert_kernel(be_ref, na_ref, tok_ref, x_ref, wg_ref, wu_ref, wd_ref, o_ref, buf_ref, sem):
    blk = pl.program_id(0)
    rows = buf_ref.shape[1]
    n_active = na_ref[0]

    def row_copy(b, r):
        slot = b % 2
        tok = tok_ref[b * rows + r]
        return pltpu.make_async_copy(x_ref.at[pl.ds(tok, 1), :], buf_ref.at[slot, pl.ds(r, 1), :], sem.at[slot])

    def gather(b):
        def start(r, c):
            row_copy(b, r).start()
            return c

        lax.fori_loop(0, rows, start, 0, unroll=GATHER_UNROLL)

    @pl.when(jnp.logical_and(blk == 0, n_active > 0))
    def _():
        gather(blk)

    @pl.when(blk + 1 < n_active)
    def _():
        gather(blk + 1)

    @pl.when(blk < n_active)
    def _():
        def wait(r, c):
            row_copy(blk, r).wait()
            return c

        lax.fori_loop(0, rows, wait, 0, unroll=GATHER_UNROLL)
        x = buf_ref[blk % 2].astype(BF16)
        gate = jnp.dot(x, wg_ref[...], preferred_element_type=F32)
        up = jnp.dot(x, wu_ref[...], preferred_element_type=F32)
        act = (gate * jax.nn.sigmoid(gate) * up).astype(BF16)
        o_ref[...] = jnp.dot(act, wd_ref[...], preferred_element_type=F32)

    @pl.when(blk >= na_ref[0])
    def _():
        o_ref[...] = jnp.zeros_like(o_ref)


def run_experts(block_expert, n_active, row_token, x, w_gate, w_up, w_down):
    n_blocks = block_expert.shape[0]
    _, d, ff = w_gate.shape
    rows = MOE_ROWS
    wspec = lambda a, b: pl.BlockSpec((None, a, b), lambda i, be, na, tk: (be[i], 0, 0))
    return pl.pallas_call(
        _expert_kernel,
        grid_spec=pltpu.PrefetchScalarGridSpec(
            num_scalar_prefetch=3,
            grid=(n_blocks,),
            in_specs=[pl.BlockSpec(memory_space=pl.ANY), wspec(d, ff), wspec(d, ff), wspec(ff, d)],
            out_specs=pl.BlockSpec((rows, d), lambda i, be, na, tk: (i, 0)),
            scratch_shapes=[pltpu.VMEM((2, rows, d), F32), pltpu.SemaphoreType.DMA((2,))],
        ),
        out_shape=jax.ShapeDtypeStruct((n_blocks * rows, d), F32),
        compiler_params=_params("arbitrary"),
        name="run_experts",
    )(block_expert, n_active, row_token, x, w_gate, w_up, w_down)


def _combine_kernel(dest_ref, h_ref, wt_ref, p_ref, gp_ref, wpg_ref, wpp_ref, gf_ref, y_ref, o_ref, buf_ref, sem):
    i = pl.program_id(0)
    tm = h_ref.shape[0]

    def row_copy(t, r, c):
        slot = t % 2
        src = dest_ref[(t * tm + r) * TOP_K + c]
        return pltpu.make_async_copy(y_ref.at[pl.ds(src, 1), :], buf_ref.at[slot, c, pl.ds(r, 1), :], sem.at[slot, c])

    def gather(t):
        def start(r, carry):
            for c in range(TOP_K):
                row_copy(t, r, c).start()
            return carry

        lax.fori_loop(0, tm, start, 0, unroll=GATHER_UNROLL)

    @pl.when(i == 0)
    def _():
        gather(i)

    @pl.when(i + 1 < pl.num_programs(0))
    def _():
        gather(i + 1)

    def wait(r, carry):
        for c in range(TOP_K):
            row_copy(i, r, c).wait()
        return carry

    lax.fori_loop(0, tm, wait, 0, unroll=GATHER_UNROLL)
    wt = wt_ref[...]
    slot = i % 2
    h2 = h_ref[...] + (wt[:, 0:1] * buf_ref[slot, 0] + wt[:, 1:2] * buf_ref[slot, 1])
    gate = jax.nn.sigmoid(jnp.dot(_rms(h2, gp_ref[...]).astype(BF16), wpg_ref[...], preferred_element_type=F32))
    ple = jnp.dot(p_ref[...].astype(BF16), wpp_ref[...], preferred_element_type=F32)
    o_ref[...] = _rms(h2 + gate * ple, gf_ref[...])


def combine_and_finish(dest, h1, wts, p, g_ple, w_ple_gate, w_ple_proj, g_final, y_rows):
    t, d = h1.shape
    tm = min(256, t)
    row = lambda n: pl.BlockSpec((tm, n), lambda i, ds: (i, 0))
    full = lambda a: pl.BlockSpec(a.shape, lambda i, ds: (0, 0))
    gp, gf = g_ple.reshape(1, d), g_final.reshape(1, d)
    return pl.pallas_call(
        _combine_kernel,
        grid_spec=pltpu.PrefetchScalarGridSpec(
            num_scalar_prefetch=1,
            grid=(t // tm,),
            in_specs=[row(d), row(LANES), row(p.shape[1]), full(gp), full(w_ple_gate), full(w_ple_proj), full(gf),
                      pl.BlockSpec(memory_space=pl.ANY)],
            out_specs=row(d),
            scratch_shapes=[pltpu.VMEM((2, TOP_K, tm, d), F32), pltpu.SemaphoreType.DMA((2, TOP_K))],
        ),
        out_shape=jax.ShapeDtypeStruct((t, d), F32),
        compiler_params=_params("arbitrary"),
        name="combine_and_finish",
    )(dest, h1, wts, p, gp, w_ple_gate, w_ple_proj, gf, y_rows)


def _segments(d):
    qk = d // 2
    sizes = dict(gq=qk, gk=qk, gv=d, gr=d, ga=GLA_RANK, fq=d, fk=d, fv=d, ff=d // FOX_DH, mg=d, mf=d)
    start, acc = {}, 0
    for name, size in sizes.items():
        start[name] = acc
        acc += size
    return sizes, start


def _project(x, g_mix, w_main, w_k, w_v, w_small, w_a2_pad, b_a):
    u = rms_cast(x, g_mix)
    return (matmul(u, w_main), matmul(u, w_k), matmul(u, w_v),
            *_small(matmul(u, w_small, tn=LANES), w_a2_pad, b_a))


def _small(z_small, w_a2_pad, b_a):
    return z_small, gla_log_decay(z_small, w_a2_pad, b_a)


def kernel(x_prompt, x_sample, cache_fox_k, cache_fox_v, cache_fox_logf, state_gla, page_table,
           p_prompt, p_sample, g_mix, w_in, w_gla_a2, b_gla_a, b_fox_f, g_gla_out, w_o_gla, w_o_fox, w_out,
           g_ffn, w_route_group, b_route_group, w_route_expert, b_route_expert, w_exp_gate, w_exp_up,
           w_exp_down, g_ple, w_ple_gate, w_ple_proj, g_final):
    depth = w_in.shape[0]
    bp, sp, d = x_prompt.shape
    bs, ss, _ = x_sample.shape
    assert ss == 1, "the sample group carries one new token per sequence"
    fox_heads = d // FOX_DH
    n_phys, page_size = cache_fox_k.shape[1:3]
    tp, ts = bp * sp, bs * ss
    sizes, start = _segments(d)
    main = ("gq", "gk", "gv", "gr", "fq", "mg", "mf")
    off, acc = {}, 0
    for name in main:
        off[name] = acc
        acc += sizes[name]

    hp = x_prompt.reshape(tp, d)
    hs = x_sample.reshape(ts, d)
    outs = {name: [] for name in ("kp", "vp", "lfp", "sp", "ks", "vs", "lfs", "ss")}
    for i in range(depth):
        w = w_in[i]
        seg = lambda name: w[:, start[name]:start[name] + sizes[name]]
        w_main = jnp.concatenate([seg(n) for n in main], axis=1).astype(BF16)
        w_k, w_v = seg("fk").astype(BF16), seg("fv").astype(BF16)
        w_small = jnp.pad(jnp.concatenate([seg("ga"), seg("ff")], axis=1),
                          ((0, 0), (0, LANES - GLA_RANK - fox_heads))).astype(BF16)
        w_a2_pad = jnp.pad(w_gla_a2[i], ((0, LANES - GLA_RANK), (0, 0))).astype(BF16)
        w_router = jnp.pad(jnp.concatenate([w_route_group[i], w_route_expert[i]], axis=1),
                           ((0, 0), (0, LANES - N_GROUPS - N_EXPERTS))).astype(BF16)
        b_router = jnp.pad(jnp.concatenate([b_route_group[i], b_route_expert[i]]),
                           (0, LANES - N_GROUPS - N_EXPERTS))
        wog, wof, wo = w_o_gla[i].astype(BF16), w_o_fox[i].astype(BF16), w_out[i].astype(BF16)
        proj = (g_mix[i], w_main, w_k, w_v, w_small, w_a2_pad, b_gla_a[i])

        def forget(z_small, batch, seq):
            f_t = z_small[:, GLA_RANK:GLA_RANK + fox_heads].reshape(batch, seq, fox_heads).transpose(0, 2, 1)
            return fox_log_forget(f_t, b_fox_f[i])

        zp, fkp, fvp, zsp, lap = _project(hp, *proj)
        lf_t, c_t = forget(zsp, bp, sp)
        ogp, s_fin = gla_prompt(zp, lap, g_gla_out[i], bp, sp, off)
        ofp = fox_prompt(zp, fkp, fvp, c_t.reshape(bp * fox_heads, 1, sp), bp, sp, off["fq"])
        mixp = merge_branches(ogp, ofp, wog, wof, zp, off["mg"], off["mf"])
        h1p, hnp, idp, wtp = out_proj_and_route(mixp, wo, hp, g_ffn[i], w_router, b_router)
        outs["kp"].append(fkp.reshape(bp, sp, fox_heads, FOX_DH))
        outs["vp"].append(fvp.reshape(bp, sp, fox_heads, FOX_DH))
        outs["lfp"].append(lf_t.transpose(0, 2, 1))
        outs["sp"].append(s_fin)

        zs, fks, fvs, zss, las = _project(hs, *proj)
        lfs_t, _ = forget(zss, 1, ts)
        lfs = lfs_t[0].T
        ogs, s_new = gla_sample(zs, las, g_gla_out[i], state_gla[i], off)
        bias = fox_sample_bias(page_table, lfs.reshape(bs, fox_heads, 1),
                               cache_fox_logf[i].transpose(0, 2, 1))
        n_pages = page_table.shape[1]
        bias = bias.transpose(0, 1, 3, 2).reshape(bs, n_pages // FOX_PAGES_PER_STEP, FOX_PAGES_PER_STEP,
                                                  page_size * fox_heads)
        heads = lambda a: a.reshape(bs, fox_heads, FOX_DH)
        ofs = fox_sample(page_table, heads(zs[:, off["fq"]:off["fq"] + d]), heads(fks), heads(fvs), bias,
                         cache_fox_k, cache_fox_v, i).reshape(bs, d)
        mixs = merge_branches(ogs, ofs, wog, wof, zs, off["mg"], off["mf"])
        h1s, hns, ids, wts = out_proj_and_route(mixs, wo, hs, g_ffn[i], w_router, b_router)
        outs["ks"].append(fks.reshape(bs, ss, fox_heads, FOX_DH))
        outs["vs"].append(fvs.reshape(bs, ss, fox_heads, FOX_DH))
        outs["lfs"].append(lfs.reshape(bs, ss, fox_heads))
        outs["ss"].append(s_new)

        t_all = tp + ts
        n_slots = t_all * TOP_K
        t_pad = -(-t_all // 512) * 512
        ids_all = jnp.concatenate([idp, ids], axis=0)
        ids_pad = jnp.pad(ids_all, ((0, t_pad - t_all), (0, 0)), constant_values=LANES - 1)
        rank, counts = rank_slots(ids_pad)
        counts = counts[0, :N_EXPERTS].astype(I32)
        padded = -(-counts // MOE_ROWS) * MOE_ROWS
        ends = jnp.cumsum(padded)
        experts = ids_all[:, :TOP_K]
        dest = (ends - padded)[experts] + rank[:t_all, :TOP_K]
        n_blocks = -(-n_slots // MOE_ROWS) + N_EXPERTS
        n_active = (ends[-1] // MOE_ROWS).reshape(1)
        block_start = jnp.arange(n_blocks, dtype=I32) * MOE_ROWS
        block_expert = jnp.searchsorted(ends, jnp.minimum(block_start, ends[-1] - 1), side="right").astype(I32)
        tokens = jnp.broadcast_to(jnp.arange(t_all, dtype=I32)[:, None], (t_all, TOP_K))
        row_token = jnp.zeros((n_blocks * MOE_ROWS,), I32).at[dest.reshape(-1)].set(tokens.reshape(-1))
        y_rows = run_experts(block_expert, n_active.astype(I32), row_token, jnp.concatenate([hnp, hns], axis=0),
                             w_exp_gate[i].astype(BF16), w_exp_up[i].astype(BF16), w_exp_down[i].astype(BF16))

        last = i == depth - 1
        assert last, "the final norm is fused into the last layer's finishing kernel"
        fin = (g_ple[i], w_ple_gate[i].astype(BF16), w_ple_proj[i].astype(BF16), g_final, y_rows)
        hp = combine_and_finish(dest[:tp].reshape(-1), h1p, wtp, p_prompt[i].reshape(tp, -1), *fin)
        hs = combine_and_finish(dest[tp:].reshape(-1), h1s, wts, p_sample[i].reshape(ts, -1), *fin)

    stack = lambda name: jnp.stack(outs[name])
    return (hp.reshape(bp, sp, d), hs.reshape(bs, ss, d),
            stack("kp"), stack("vp"), stack("lfp"), stack("sp"),
            stack("ks"), stack("vs"), stack("lfs"), stack("ss"))
```

```python
import functools

import jax
import jax.numpy as jnp
from jax import lax
from jax.experimental import pallas as pl
from jax.experimental.pallas import tpu as pltpu

F32 = jnp.float32
BF16 = jnp.bfloat16
I32 = jnp.int32
HIGHEST = lax.Precision.HIGHEST
NEG_INF = float("-inf")

RMS_EPS = 1e-6
GLA_HEADS = 4
GLA_RANK = 16
GLA_TAU = 16.0
GLA_CHUNK = 64
GLA_SUB = 16
FOX_DH = 128
N_GROUPS = 4
EXPERTS_PER_GROUP = 8
N_EXPERTS = N_GROUPS * EXPERTS_PER_GROUP
TOP_K = 2

LANES = 128
MOE_ROWS = 256
GATHER_UNROLL = 8
VMEM_LIMIT = 56 * 2**20


def _params(*sem):
    return pltpu.CompilerParams(dimension_semantics=sem, vmem_limit_bytes=VMEM_LIMIT)


def _log_sigmoid(x):
    return jnp.minimum(x, 0.0) - jnp.log1p(jnp.exp(-jnp.abs(x)))


def _rms(x, g):
    return x * lax.rsqrt(jnp.mean(x * x, axis=-1, keepdims=True) + RMS_EPS) * g


def _nt_dot(a, b):
    return lax.dot_general(a, b, (((1,), (1,)), ((), ())), preferred_element_type=F32)


def _tn_dot(a, b):
    return lax.dot_general(a, b, (((0,), (0,)), ((), ())), preferred_element_type=F32)


def _row_to_col(row):
    n = row.shape[1]
    eye = lax.broadcasted_iota(I32, (n, n), 0) == lax.broadcasted_iota(I32, (n, n), 1)
    return jnp.sum(jnp.where(eye, row, 0.0), axis=1, keepdims=True)


def _rms_cast_kernel(x_ref, g_ref, o_ref):
    o_ref[...] = _rms(x_ref[...], g_ref[...]).astype(o_ref.dtype)


def rms_cast(x, g):
    t, d = x.shape
    tm = min(512, t)
    return pl.pallas_call(
        _rms_cast_kernel,
        grid=(t // tm,),
        in_specs=[pl.BlockSpec((tm, d), lambda i: (i, 0)), pl.BlockSpec((1, d), lambda i: (0, 0))],
        out_specs=pl.BlockSpec((tm, d), lambda i: (i, 0)),
        out_shape=jax.ShapeDtypeStruct((t, d), BF16),
        compiler_params=_params("parallel"),
        name="rms_cast",
    )(x, g.reshape(1, d))


def _mm_kernel(a_ref, w_ref, o_ref):
    o_ref[...] = jnp.dot(a_ref[...], w_ref[...], preferred_element_type=F32).astype(o_ref.dtype)


def matmul(a, w, out_dtype=F32, tm=512, tn=1024):
    t, k = a.shape
    n = w.shape[1]
    tm, tn = min(tm, t), min(tn, n)
    return pl.pallas_call(
        _mm_kernel,
        grid=(n // tn, t // tm),
        in_specs=[pl.BlockSpec((tm, k), lambda j, i: (i, 0)), pl.BlockSpec((k, tn), lambda j, i: (0, j))],
        out_specs=pl.BlockSpec((tm, tn), lambda j, i: (i, j)),
        out_shape=jax.ShapeDtypeStruct((t, n), out_dtype),
        compiler_params=_params("parallel", "parallel"),
        name="matmul",
    )(a, w)


def _la_kernel(z_ref, w_ref, b_ref, o_ref):
    pre = jnp.dot(z_ref[...].astype(BF16), w_ref[...], preferred_element_type=F32) + b_ref[...]
    o_ref[...] = _log_sigmoid(pre) / GLA_TAU


def gla_log_decay(z_small, w_a2_pad, b_a):
    t = z_small.shape[0]
    n = w_a2_pad.shape[1]
    tm = min(512, t)
    return pl.pallas_call(
        _la_kernel,
        grid=(t // tm,),
        in_specs=[pl.BlockSpec((tm, LANES), lambda i: (i, 0)),
                  pl.BlockSpec((LANES, n), lambda i: (0, 0)),
                  pl.BlockSpec((1, n), lambda i: (0, 0))],
        out_specs=pl.BlockSpec((tm, n), lambda i: (i, 0)),
        out_shape=jax.ShapeDtypeStruct((t, n), F32),
        compiler_params=_params("parallel"),
        name="gla_log_decay",
    )(z_small, w_a2_pad, b_a.reshape(1, n))


def _logf_kernel(f_ref, b_ref, lf_ref, c_ref, *, blk):
    lf = _log_sigmoid(f_ref[...] + b_ref[...])
    lf_ref[...] = lf
    s = lf.shape[1]
    upper = (lax.broadcasted_iota(I32, (blk, blk), 0) <= lax.broadcasted_iota(I32, (blk, blk), 1)).astype(F32)
    carry = jnp.zeros((lf.shape[0], 1), F32)
    for j in range(s // blk):
        cs = jnp.dot(lf[:, j * blk:(j + 1) * blk], upper, precision=HIGHEST, preferred_element_type=F32) + carry
        c_ref[:, j * blk:(j + 1) * blk] = cs
        carry = cs[:, blk - 1:blk]


def fox_log_forget(f_t, b_f):
    b, h, s = f_t.shape
    blk = min(256, s)
    spec = pl.BlockSpec((None, h, s), lambda i: (i, 0, 0))
    return pl.pallas_call(
        functools.partial(_logf_kernel, blk=blk),
        grid=(b,),
        in_specs=[spec, pl.BlockSpec((h, 1), lambda i: (0, 0))],
        out_specs=[spec, spec],
        out_shape=[jax.ShapeDtypeStruct((b, h, s), F32)] * 2,
        compiler_params=_params("parallel"),
        name="fox_log_forget",
    )(f_t, b_f.reshape(h, 1))


def _gla_prompt_kernel(q_ref, k_ref, v_ref, r_ref, la_ref, g_ref, o_ref, s_ref, *, rows, heads):
    @pl.when(pl.program_id(2) == 0)
    def _():
        s_ref[...] = jnp.zeros_like(s_ref)

    c, sub = GLA_CHUNK, GLA_SUB
    nsub = c // sub
    dk = q_ref.shape[1] // heads
    dv = v_ref.shape[1] // heads
    scale = dk ** -0.5
    row = lax.broadcasted_iota(I32, (c, c), 0)
    col = lax.broadcasted_iota(I32, (c, c), 1)
    lower = (row >= col).astype(F32)
    col_sub = lax.broadcasted_iota(I32, (sub, c), 1)
    row_in_sub = lax.broadcasted_iota(I32, (c, 1), 0) % sub

    def sub_rows(x, s):
        return jnp.concatenate(
            [jnp.broadcast_to(x[i * sub + s:i * sub + s + 1, :], (sub, dk)) for i in range(nsub)], axis=0)

    def one_head(sl, hh):
        kcols = slice(hh * dk, (hh + 1) * dk)
        vcols = slice(hh * dv, (hh + 1) * dv)
        q = q_ref[sl, kcols] * scale
        k = k_ref[sl, kcols]
        vb = v_ref[sl, vcols].astype(BF16)
        b = jnp.dot(lower, la_ref[sl, kcols], precision=HIGHEST, preferred_element_type=F32)
        state = s_ref[hh]

        o = jnp.dot((q * jnp.exp(b)).astype(BF16), state.astype(BF16), preferred_element_type=F32)

        a = jnp.zeros((c, c), F32)
        for s in range(sub):
            e = jnp.exp(jnp.where(row_in_sub >= s, b - sub_rows(b, s), NEG_INF))
            w = jnp.sum(q * e * sub_rows(k, s), axis=-1, keepdims=True)
            a = jnp.where(col == (row // sub) * sub + s, w, a)
        blocks = [a[:sub]]
        for i in range(1, nsub):
            ri = slice(i * sub, (i + 1) * sub)
            edge = b[i * sub - 1:i * sub, :]
            qs = (q[ri] * jnp.exp(b[ri] - edge)).astype(BF16)
            ks = (k * jnp.exp(jnp.minimum(edge - b, 0.0))).astype(BF16)
            blocks.append(jnp.where(col_sub < i * sub, _nt_dot(qs, ks), a[ri]))
        a = jnp.concatenate(blocks, axis=0)
        o = o + jnp.dot(a.astype(BF16), vb, preferred_element_type=F32)

        r = r_ref[sl, vcols]
        o_ref[sl, vcols] = (_rms(o, g_ref[...]) * (r * jax.nn.sigmoid(r))).astype(o_ref.dtype)

        last = b[c - 1:c, :]
        kd = (k * jnp.exp(last - b)).astype(BF16)
        s_ref[hh] = _row_to_col(jnp.exp(last)) * state + _tn_dot(kd, vb)

    def chunk(ci, carry):
        sl = pl.ds(pl.multiple_of(ci * c, c), c)
        for hh in range(heads):
            one_head(sl, hh)
        return carry

    lax.fori_loop(0, rows // c, chunk, 0)


GLA_HEADS_PER_STEP = 2


def gla_prompt(z, la, g_out, batch, seq, off):
    t = z.shape[0]
    hp = GLA_HEADS_PER_STEP
    dk = la.shape[1] // GLA_HEADS
    dv = g_out.shape[0]
    wk, wv = hp * dk, hp * dv
    rows = min(512, seq)
    nt = seq // rows
    kern = functools.partial(_gla_prompt_kernel, rows=rows, heads=hp)
    rmap = lambda b, h, i: b * nt + i
    return pl.pallas_call(
        kern,
        grid=(batch, GLA_HEADS // hp, nt),
        in_specs=[
            pl.BlockSpec((rows, wk), lambda b, h, i: (rmap(b, h, i), off["gq"] // wk + h)),
            pl.BlockSpec((rows, wk), lambda b, h, i: (rmap(b, h, i), off["gk"] // wk + h)),
            pl.BlockSpec((rows, wv), lambda b, h, i: (rmap(b, h, i), off["gv"] // wv + h)),
            pl.BlockSpec((rows, wv), lambda b, h, i: (rmap(b, h, i), off["gr"] // wv + h)),
            pl.BlockSpec((rows, wk), lambda b, h, i: (rmap(b, h, i), h)),
            pl.BlockSpec((1, dv), lambda b, h, i: (0, 0)),
        ],
        out_specs=[
            pl.BlockSpec((rows, wv), lambda b, h, i: (rmap(b, h, i), h)),
            pl.BlockSpec((None, hp, dk, dv), lambda b, h, i: (b, h, 0, 0)),
        ],
        out_shape=[jax.ShapeDtypeStruct((t, GLA_HEADS * dv), BF16),
                   jax.ShapeDtypeStruct((batch, GLA_HEADS, dk, dv), F32)],
        compiler_params=_params("parallel", "parallel", "arbitrary"),
        name="gla_prompt",
    )(z, z, z, z, la, g_out.reshape(1, dv))


def _gla_sample_kernel(q_ref, k_ref, v_ref, r_ref, la_ref, g_ref, s_ref, o_ref, sn_ref):
    dk = q_ref.shape[1]
    q = q_ref[...] * dk ** -0.5
    k = k_ref[...]
    v = v_ref[...]
    decay = jnp.exp(la_ref[...])
    state = s_ref[...]
    sn_ref[...] = _row_to_col(decay) * state + _row_to_col(k) * v
    qe = _row_to_col((q * decay).astype(BF16).astype(F32))
    o = jnp.sum(qe * state.astype(BF16).astype(F32), axis=0, keepdims=True)
    o = o + jnp.sum(q * k, axis=1, keepdims=True) * v
    r = r_ref[...]
    o_ref[...] = (_rms(o, g_ref[...]) * (r * jax.nn.sigmoid(r))).astype(o_ref.dtype)


def gla_sample(z, la, g_out, state, off):
    b = z.shape[0]
    _, h, dk, dv = state.shape
    z3 = z.reshape(b, 1, z.shape[1])
    la3 = la.reshape(b, 1, la.shape[1])
    og, sn = pl.pallas_call(
        _gla_sample_kernel,
        grid=(b, h),
        in_specs=[
            pl.BlockSpec((None, 1, dk), lambda i, j: (i, 0, off["gq"] // dk + j)),
            pl.BlockSpec((None, 1, dk), lambda i, j: (i, 0, off["gk"] // dk + j)),
            pl.BlockSpec((None, 1, dv), lambda i, j: (i, 0, off["gv"] // dv + j)),
            pl.BlockSpec((None, 1, dv), lambda i, j: (i, 0, off["gr"] // dv + j)),
            pl.BlockSpec((None, 1, dk), lambda i, j: (i, 0, j)),
            pl.BlockSpec((1, dv), lambda i, j: (0, 0)),
            pl.BlockSpec((None, None, dk, dv), lambda i, j: (i, j, 0, 0)),
        ],
        out_specs=[
            pl.BlockSpec((None, 1, dv), lambda i, j: (i, 0, j)),
            pl.BlockSpec((None, None, dk, dv), lambda i, j: (i, j, 0, 0)),
        ],
        out_shape=[jax.ShapeDtypeStruct((b, 1, h * dv), BF16), jax.ShapeDtypeStruct(state.shape, F32)],
        compiler_params=_params("parallel", "parallel"),
        name="gla_sample",
    )(z3, z3, z3, z3, la3, g_out.reshape(1, dv), state)
    return og.reshape(b, h * dv), sn


def _fox_prompt_kernel(q_ref, k_ref, v_ref, c_ref, o_ref, kb_ref, vb_ref, *, blk):
    s, dh = q_ref.shape
    kb_ref[...] = k_ref[...].astype(BF16)
    vb_ref[...] = v_ref[...].astype(BF16)
    causal = lax.broadcasted_iota(I32, (blk, blk), 0) >= lax.broadcasted_iota(I32, (blk, blk), 1)
    for i in range(s // blk):
        qs = slice(i * blk, (i + 1) * blk)
        q = (q_ref[qs, :] * dh ** -0.5).astype(BF16)
        m = jnp.full((blk, 1), NEG_INF, F32)
        l = jnp.zeros((blk, 1), F32)
        acc = jnp.zeros((blk, dh), F32)
        for j in range(i + 1):
            ks = slice(j * blk, (j + 1) * blk)
            logit = _nt_dot(q, kb_ref[ks, :]) - c_ref[:, ks]
            if j == i:
                logit = jnp.where(causal, logit, NEG_INF)
            m_new = jnp.maximum(m, jnp.max(logit, axis=-1, keepdims=True))
            alpha = jnp.exp(m - m_new)
            p = jnp.exp(logit - m_new)
            l = alpha * l + jnp.sum(p, axis=-1, keepdims=True)
            acc = alpha * acc + jnp.dot(p.astype(BF16), vb_ref[ks, :], preferred_element_type=F32)
            m = m_new
        o_ref[qs, :] = (acc / l).astype(o_ref.dtype)


def fox_prompt(z, fk, fv, c_t, batch, seq, q_off):
    t, w = fk.shape
    h = w // FOX_DH
    blk = min(512, seq)
    return pl.pallas_call(
        functools.partial(_fox_prompt_kernel, blk=blk),
        grid=(batch, h),
        in_specs=[
            pl.BlockSpec((seq, FOX_DH), lambda b, j: (b, q_off // FOX_DH + j)),
            pl.BlockSpec((seq, FOX_DH), lambda b, j: (b, j)),
            pl.BlockSpec((seq, FOX_DH), lambda b, j: (b, j)),
            pl.BlockSpec((None, 1, seq), lambda b, j: (b * h + j, 0, 0)),
        ],
        out_specs=pl.BlockSpec((seq, FOX_DH), lambda b, j: (b, j)),
        out_shape=jax.ShapeDtypeStruct((t, w), BF16),
        scratch_shapes=[pltpu.VMEM((seq, FOX_DH), BF16)] * 2,
        compiler_params=_params("parallel", "parallel"),
        name="fox_prompt",
    )(z, fk, fv, c_t)


def _fox_bias_kernel(pt_ref, lfn_ref, pool_ref, o_ref, buf_ref, sem):
    b = pl.program_id(0)
    npg, h, ps = buf_ref.shape

    def page_copy(p):
        return pltpu.make_async_copy(pool_ref.at[pt_ref[b, p]], buf_ref.at[p], sem)

    for p in range(npg):
        page_copy(p).start()
    for p in range(npg):
        page_copy(p).wait()

    x = buf_ref[...]
    after = (lax.broadcasted_iota(I32, (ps, ps), 0) >= lax.broadcasted_iota(I32, (ps, ps), 1)).astype(F32)
    suffix = jnp.dot(x.reshape(npg * h, ps), after, precision=HIGHEST,
                     preferred_element_type=F32).reshape(npg, h, ps)
    run = lfn_ref[...]
    for p in range(npg - 1, -1, -1):
        o_ref[p] = suffix[p] - x[p] + run
        run = run + suffix[p][:, 0:1]


def fox_sample_bias(page_table, lf_new, pool_lf_t):
    b, npg = page_table.shape
    _, h, ps = pool_lf_t.shape
    return pl.pallas_call(
        _fox_bias_kernel,
        grid_spec=pltpu.PrefetchScalarGridSpec(
            num_scalar_prefetch=1,
            grid=(b,),
            in_specs=[pl.BlockSpec((None, h, 1), lambda i, pt: (i, 0, 0)),
                      pl.BlockSpec(memory_space=pl.ANY)],
            out_specs=pl.BlockSpec((None, npg, h, ps), lambda i, pt: (i, 0, 0, 0)),
            scratch_shapes=[pltpu.VMEM((npg, h, ps), F32), pltpu.SemaphoreType.DMA(())],
        ),
        out_shape=jax.ShapeDtypeStruct((b, npg, h, ps), F32),
        compiler_params=_params("arbitrary"),
        name="fox_sample_bias",
    )(page_table, lf_new, pool_lf_t)


def _fox_sample_kernel(pt_ref, q_ref, kn_ref, vn_ref, bias_ref, *refs, pages):
    k_refs, v_refs = refs[:pages], refs[pages:2 * pages]
    o_ref, m_ref, l_ref, acc_ref = refs[2 * pages:]
    p = pl.program_id(1)
    h, dh = q_ref.shape
    n = k_refs[0].shape[0] * h
    q = q_ref[...] * dh ** -0.5

    @pl.when(p == 0)
    def _():
        m_ref[...] = jnp.broadcast_to(jnp.sum(q * kn_ref[...], axis=1, keepdims=True), m_ref.shape)
        l_ref[...] = jnp.ones_like(l_ref)
        acc_ref[...] = vn_ref[...]

    own = lax.broadcasted_iota(I32, (h, n), 1) % h == lax.broadcasted_iota(I32, (h, n), 0)
    qb = q.astype(BF16)
    logits = [jnp.where(own, _nt_dot(qb, k_refs[r][...].reshape(n, dh).astype(BF16)) + bias_ref[r:r + 1, :], NEG_INF)
              for r in range(pages)]
    m = m_ref[:, 0:1]
    m_new = m
    for lg in logits:
        m_new = jnp.maximum(m_new, jnp.max(lg, axis=-1, keepdims=True))
    alpha = jnp.exp(m - m_new)
    l = alpha * l_ref[:, 0:1]
    acc = alpha * acc_ref[...]
    for r in range(pages):
        pr = jnp.exp(logits[r] - m_new)
        l = l + jnp.sum(pr, axis=-1, keepdims=True)
        acc = acc + jnp.dot(pr.astype(BF16), v_refs[r][...].reshape(n, dh).astype(BF16), preferred_element_type=F32)
    m_ref[...] = jnp.broadcast_to(m_new, m_ref.shape)
    l_ref[...] = jnp.broadcast_to(l, l_ref.shape)
    acc_ref[...] = acc

    @pl.when(p == pl.num_programs(1) - 1)
    def _():
        o_ref[...] = (acc / l).astype(o_ref.dtype)


FOX_PAGES_PER_STEP = 4


def fox_sample(page_table, q, k_new, v_new, bias, cache_k, cache_v, layer):
    b, npg = page_table.shape
    _, _, ps, h, dh = cache_k.shape
    g = FOX_PAGES_PER_STEP
    row = pl.BlockSpec((None, h, dh), lambda i, p, pt: (i, 0, 0))

    def page(r):
        return pl.BlockSpec((None, None, ps, h, dh), lambda i, p, pt: (layer, pt[i, p * g + r], 0, 0, 0))

    pages = [page(r) for r in range(g)]
    return pl.pallas_call(
        functools.partial(_fox_sample_kernel, pages=g),
        grid_spec=pltpu.PrefetchScalarGridSpec(
            num_scalar_prefetch=1,
            grid=(b, npg // g),
            in_specs=[row, row, row,
                      pl.BlockSpec((None, None, g, ps * h), lambda i, p, pt: (i, p, 0, 0)),
                      *pages, *pages],
            out_specs=row,
            scratch_shapes=[pltpu.VMEM((h, LANES), F32), pltpu.VMEM((h, LANES), F32), pltpu.VMEM((h, dh), F32)],
        ),
        out_shape=jax.ShapeDtypeStruct((b, h, dh), BF16),
        compiler_params=_params("parallel", "arbitrary"),
        name="fox_sample",
    )(page_table, q, k_new, v_new, bias, *([cache_k] * g), *([cache_v] * g))


def _merge_kernel(og_ref, of_ref, wg_ref, wf_ref, mg_ref, mf_ref, o_ref):
    yg = jnp.dot(og_ref[...], wg_ref[...], preferred_element_type=F32)
    yf = jnp.dot(of_ref[...], wf_ref[...], preferred_element_type=F32)
    o_ref[...] = (jax.nn.sigmoid(mg_ref[...]) * yg + jax.nn.sigmoid(mf_ref[...]) * yf).astype(o_ref.dtype)


def merge_branches(og, of, w_o_gla, w_o_fox, z, mg_off, mf_off):
    t, k = og.shape
    n = w_o_gla.shape[1]
    tm, tn = min(512, t), min(1024, n)
    return pl.pallas_call(
        _merge_kernel,
        grid=(n // tn, t // tm),
        in_specs=[pl.BlockSpec((tm, k), lambda j, i: (i, 0)),
                  pl.BlockSpec((tm, k), lambda j, i: (i, 0)),
                  pl.BlockSpec((k, tn), lambda j, i: (0, j)),
                  pl.BlockSpec((k, tn), lambda j, i: (0, j)),
                  pl.BlockSpec((tm, tn), lambda j, i: (i, mg_off // tn + j)),
                  pl.BlockSpec((tm, tn), lambda j, i: (i, mf_off // tn + j))],
        out_specs=pl.BlockSpec((tm, tn), lambda j, i: (i, j)),
        out_shape=jax.ShapeDtypeStruct((t, n), BF16),
        compiler_params=_params("parallel", "parallel"),
        name="merge_branches",
    )(og, of, w_o_gla, w_o_fox, z, z)


def _route_kernel(mix_ref, w_ref, x_ref, g_ref, wr_ref, br_ref, *refs):
    h_ref, hn_ref, id_ref, wt_ref = refs[-4:]
    h1 = x_ref[...] + jnp.dot(mix_ref[...], w_ref[...], preferred_element_type=F32)
    h_ref[...] = h1
    hn = _rms(h1, g_ref[...])
    hn_ref[...] = hn
    logits = jnp.dot(hn.astype(BF16), wr_ref[...], preferred_element_type=F32) + br_ref[...]
    lane = lax.broadcasted_iota(I32, logits.shape, 1)
    big = jnp.int32(2 ** 30)

    def top(vals):
        best = jnp.max(vals, axis=-1, keepdims=True)
        return best, jnp.min(jnp.where(vals == best, lane, big), axis=-1, keepdims=True)

    grp = jnp.where(lane < N_GROUPS, logits, NEG_INF)
    g_max, g_idx = top(grp)
    g_w = 1.0 / jnp.sum(jnp.exp(grp - g_max), axis=-1, keepdims=True)
    lo = N_GROUPS + g_idx * EXPERTS_PER_GROUP
    mine = jnp.where(lane >= lo, jnp.where(lane < lo + EXPERTS_PER_GROUP, logits, NEG_INF), NEG_INF)
    e1, i1 = top(mine)
    e2, i2 = top(jnp.where(lane == i1, NEG_INF, mine))
    z = jnp.sum(jnp.exp(mine - e1), axis=-1, keepdims=True)
    p1 = 1.0 / z
    p2 = jnp.exp(e2 - e1) / z
    w1 = g_w * p1 / (p1 + p2)
    w2 = g_w * p2 / (p1 + p2)
    id_ref[...] = jnp.where(lane == 0, i1 - N_GROUPS, jnp.where(lane == 1, i2 - N_GROUPS, 0))
    wt_ref[...] = jnp.where(lane == 0, w1, jnp.where(lane == 1, w2, 0.0))


def out_proj_and_route(mix, w_out, x, g_ffn, w_router, b_router, total_rows, row_offset=0, normed=None):
    t, d = x.shape
    tm = min(256, t)
    assert row_offset % tm == 0
    row = lambda n: pl.BlockSpec((tm, n), lambda i: (i, 0))
    full = lambda a: pl.BlockSpec(a.shape, lambda i: (0, 0))
    g2, b2 = g_ffn.reshape(1, d), b_router.reshape(1, LANES)
    args = [mix, w_out, x, g2, w_router, b2]
    in_specs = [row(d), full(w_out), row(d), full(g2), full(w_router), full(b2)]
    aliases = {}
    if normed is not None:
        aliases = {len(args): 1}
        args.append(normed)
        in_specs.append(pl.BlockSpec(memory_space=pl.ANY))
    return pl.pallas_call(
        _route_kernel,
        grid=(t // tm,),
        in_specs=in_specs,
        out_specs=[row(d), pl.BlockSpec((tm, d), lambda i: (row_offset // tm + i, 0)), row(LANES), row(LANES)],
        out_shape=[jax.ShapeDtypeStruct((t, d), F32), jax.ShapeDtypeStruct((total_rows, d), F32),
                   jax.ShapeDtypeStruct((t, LANES), I32), jax.ShapeDtypeStruct((t, LANES), F32)],
        input_output_aliases=aliases,
        compiler_params=_params("parallel"),
        name="out_proj_and_route",
    )(*args)


def _rank_kernel(id_ref, rank_ref, cnt_ref):
    @pl.when(pl.program_id(0) == 0)
    def _():
        cnt_ref[...] = jnp.zeros_like(cnt_ref)

    ids = id_ref[...]
    n = ids.shape[0]
    lane = lax.broadcasted_iota(I32, ids.shape, 1)
    hit0 = lane == ids[:, 0:1]
    hit1 = lane == ids[:, 1:2]
    hits = (jnp.where(hit0, 1.0, 0.0) + jnp.where(hit1, 1.0, 0.0)).astype(BF16)
    lower = (lax.broadcasted_iota(I32, (n, n), 0) >= lax.broadcasted_iota(I32, (n, n), 1)).astype(BF16)
    seen = jnp.dot(lower, hits, preferred_element_type=F32) + cnt_ref[...]
    r0 = jnp.sum(jnp.where(hit0, seen, 0.0), axis=-1, keepdims=True) - 1.0
    r1 = jnp.sum(jnp.where(hit1, seen, 0.0), axis=-1, keepdims=True) - 1.0
    rank_ref[...] = jnp.where(lane == 0, r0, jnp.where(lane == 1, r1, 0.0)).astype(I32)
    cnt_ref[...] = seen[n - 1:n, :]


def rank_slots(ids):
    t = ids.shape[0]
    tm = 512
    return pl.pallas_call(
        _rank_kernel,
        grid=(t // tm,),
        in_specs=[pl.BlockSpec((tm, LANES), lambda i: (i, 0))],
        out_specs=[pl.BlockSpec((tm, LANES), lambda i: (i, 0)), pl.BlockSpec((1, LANES), lambda i: (0, 0))],
        out_shape=[jax.ShapeDtypeStruct((t, LANES), I32), jax.ShapeDtypeStruct((1, LANES), F32)],
        compiler_params=_params("arbitrary"),
        name="rank_slots",
    )(ids)


def _expert_kernel(be_ref, na_ref, first_ref, next_ref, tok_ref, x_ref, wg_hbm, wu_hbm, wd_hbm, o_ref,
                   buf_ref, sg_ref, su_ref, sd_ref, wg_ref, wu_ref, wd_ref, sem, wsem, *, layer):
    blk = pl.program_id(0)
    rows = buf_ref.shape[1]
    n_active = na_ref[0]

    def weight_copies(e):
        return (pltpu.make_async_copy(wg_hbm.at[layer, e], sg_ref, wsem.at[0]),
                pltpu.make_async_copy(wu_hbm.at[layer, e], su_ref, wsem.at[1]),
                pltpu.make_async_copy(wd_hbm.at[layer, e], sd_ref, wsem.at[2]))

    def row_copy(b, r):
        slot = b % 2
        tok = tok_ref[b * rows + r]
        return pltpu.make_async_copy(x_ref.at[pl.ds(tok, 1), :], buf_ref.at[slot, pl.ds(r, 1), :], sem.at[slot])

    def gather(b):
        def start(r, c):
            row_copy(b, r).start()
            return c

        lax.fori_loop(0, rows, start, 0, unroll=GATHER_UNROLL)

    @pl.when(jnp.logical_and(blk == 0, n_active > 0))
    def _():
        for cp in weight_copies(be_ref[0]):
            cp.start()
        gather(blk)

    @pl.when(blk + 1 < n_active)
    def _():
        gather(blk + 1)

    @pl.when(jnp.logical_and(blk < n_active, first_ref[blk] == 1))
    def _():
        for cp in weight_copies(be_ref[blk]):
            cp.wait()
        wg_ref[...] = sg_ref[...].astype(BF16)
        wu_ref[...] = su_ref[...].astype(BF16)
        wd_ref[...] = sd_ref[...].astype(BF16)

        @pl.when(next_ref[blk] >= 0)
        def _():
            for cp in weight_copies(next_ref[blk]):
                cp.start()

    @pl.when(blk < n_active)
    def _():
        def wait(r, c):
            row_copy(blk, r).wait()
            return c

        lax.fori_loop(0, rows, wait, 0, unroll=GATHER_UNROLL)
        x = buf_ref[blk % 2].astype(BF16)
        gate = jnp.dot(x, wg_ref[...], preferred_element_type=F32)
        up = jnp.dot(x, wu_ref[...], preferred_element_type=F32)
        act = (gate * jax.nn.sigmoid(gate) * up).astype(BF16)
        o_ref[...] = jnp.dot(act, wd_ref[...], preferred_element_type=F32)

    @pl.when(blk >= na_ref[0])
    def _():
        o_ref[...] = jnp.zeros_like(o_ref)


def run_experts(block_expert, n_active, row_token, x, w_gate, w_up, w_down, layer):
    n_blocks = block_expert.shape[0]
    _, _, d, ff = w_gate.shape
    rows = MOE_ROWS
    idx = jnp.arange(n_blocks, dtype=I32)
    first = jnp.concatenate([jnp.ones((1,), I32), (block_expert[1:] != block_expert[:-1]).astype(I32)])
    later = (idx[None, :] > idx[:, None]) & (block_expert[None, :] != block_expert[:, None]) & (idx[None, :] < n_active[0])
    nxt_blk = jnp.min(jnp.where(later, idx[None, :], n_blocks), axis=1)
    nxt = jnp.where(nxt_blk < n_blocks, block_expert[jnp.minimum(nxt_blk, n_blocks - 1)], -1).astype(I32)
    hbm = pl.BlockSpec(memory_space=pl.ANY)
    return pl.pallas_call(
        functools.partial(_expert_kernel, layer=layer),
        grid_spec=pltpu.PrefetchScalarGridSpec(
            num_scalar_prefetch=5,
            grid=(n_blocks,),
            in_specs=[hbm, hbm, hbm, hbm],
            out_specs=pl.BlockSpec((rows, d), lambda i, *_: (i, 0)),
            scratch_shapes=[pltpu.VMEM((2, rows, d), F32),
                            pltpu.VMEM((d, ff), F32), pltpu.VMEM((d, ff), F32), pltpu.VMEM((ff, d), F32),
                            pltpu.VMEM((d, ff), BF16), pltpu.VMEM((d, ff), BF16), pltpu.VMEM((ff, d), BF16),
                            pltpu.SemaphoreType.DMA((2,)), pltpu.SemaphoreType.DMA((3,))],
        ),
        out_shape=jax.ShapeDtypeStruct((n_blocks * rows, d), F32),
        compiler_params=_params("arbitrary"),
        name="run_experts",
    )(block_expert, n_active, first, nxt, row_token, x, w_gate, w_up, w_down)


def _combine_kernel(dest_ref, h_ref, wt_ref, p_ref, gp_ref, wpg_ref, wpp_ref, gf_ref, y_ref, o_ref, buf_ref, sem):
    i = pl.program_id(0)
    tm = h_ref.shape[0]

    def row_copy(t, r, c):
        slot = t % 2
        src = dest_ref[(t * tm + r) * TOP_K + c]
        return pltpu.make_async_copy(y_ref.at[pl.ds(src, 1), :], buf_ref.at[slot, c, pl.ds(r, 1), :], sem.at[slot, c])

    def gather(t):
        def start(r, carry):
            for c in range(TOP_K):
                row_copy(t, r, c).start()
            return carry

        lax.fori_loop(0, tm, start, 0, unroll=GATHER_UNROLL)

    @pl.when(i == 0)
    def _():
        gather(i)

    @pl.when(i + 1 < pl.num_programs(0))
    def _():
        gather(i + 1)

    def wait(r, carry):
        for c in range(TOP_K):
            row_copy(i, r, c).wait()
        return carry

    lax.fori_loop(0, tm, wait, 0, unroll=GATHER_UNROLL)
    wt = wt_ref[...]
    slot = i % 2
    h2 = h_ref[...] + (wt[:, 0:1] * buf_ref[slot, 0] + wt[:, 1:2] * buf_ref[slot, 1])
    gate = jax.nn.sigmoid(jnp.dot(_rms(h2, gp_ref[...]).astype(BF16), wpg_ref[...], preferred_element_type=F32))
    ple = jnp.dot(p_ref[...].astype(BF16), wpp_ref[...], preferred_element_type=F32)
    o_ref[...] = _rms(h2 + gate * ple, gf_ref[...])


def combine_and_finish(dest, h1, wts, p, g_ple, w_ple_gate, w_ple_proj, g_final, y_rows):
    t, d = h1.shape
    tm = min(256, t)
    row = lambda n: pl.BlockSpec((tm, n), lambda i, ds: (i, 0))
    full = lambda a: pl.BlockSpec(a.shape, lambda i, ds: (0, 0))
    gp, gf = g_ple.reshape(1, d), g_final.reshape(1, d)
    return pl.pallas_call(
        _combine_kernel,
        grid_spec=pltpu.PrefetchScalarGridSpec(
            num_scalar_prefetch=1,
            grid=(t // tm,),
            in_specs=[row(d), row(LANES), row(p.shape[1]), full(gp), full(w_ple_gate), full(w_ple_proj), full(gf),
                      pl.BlockSpec(memory_space=pl.ANY)],
            out_specs=row(d),
            scratch_shapes=[pltpu.VMEM((2, TOP_K, tm, d), F32), pltpu.SemaphoreType.DMA((2, TOP_K))],
        ),
        out_shape=jax.ShapeDtypeStruct((t, d), F32),
        compiler_params=_params("arbitrary"),
        name="combine_and_finish",
    )(dest, h1, wts, p, gp, w_ple_gate, w_ple_proj, gf, y_rows)


def _segments(d):
    qk = d // 2
    sizes = dict(gq=qk, gk=qk, gv=d, gr=d, ga=GLA_RANK, fq=d, fk=d, fv=d, ff=d // FOX_DH, mg=d, mf=d)
    start, acc = {}, 0
    for name, size in sizes.items():
        start[name] = acc
        acc += size
    return sizes, start


def _project(x, g_mix, w_main, w_k, w_v, w_small, w_a2_pad, b_a):
    u = rms_cast(x, g_mix)
    return (matmul(u, w_main), matmul(u, w_k), matmul(u, w_v),
            *_small(matmul(u, w_small, tn=LANES), w_a2_pad, b_a))


def _small(z_small, w_a2_pad, b_a):
    return z_small, gla_log_decay(z_small, w_a2_pad, b_a)


def kernel(x_prompt, x_sample, cache_fox_k, cache_fox_v, cache_fox_logf, state_gla, page_table,
           p_prompt, p_sample, g_mix, w_in, w_gla_a2, b_gla_a, b_fox_f, g_gla_out, w_o_gla, w_o_fox, w_out,
           g_ffn, w_route_group, b_route_group, w_route_expert, b_route_expert, w_exp_gate, w_exp_up,
           w_exp_down, g_ple, w_ple_gate, w_ple_proj, g_final):
    depth = w_in.shape[0]
    bp, sp, d = x_prompt.shape
    bs, ss, _ = x_sample.shape
    assert ss == 1, "the sample group carries one new token per sequence"
    fox_heads = d // FOX_DH
    n_phys, page_size = cache_fox_k.shape[1:3]
    tp, ts = bp * sp, bs * ss
    sizes, start = _segments(d)
    main = ("gq", "gk", "gv", "gr", "fq", "mg", "mf")
    off, acc = {}, 0
    for name in main:
        off[name] = acc
        acc += sizes[name]

    hp = x_prompt.reshape(tp, d)
    hs = x_sample.reshape(ts, d)
    outs = {name: [] for name in ("kp", "vp", "lfp", "sp", "ks", "vs", "lfs", "ss")}
    for i in range(depth):
        w = w_in[i]
        seg = lambda name: w[:, start[name]:start[name] + sizes[name]]
        w_main = jnp.concatenate([seg(n) for n in main], axis=1).astype(BF16)
        w_k, w_v = seg("fk").astype(BF16), seg("fv").astype(BF16)
        w_small = jnp.pad(jnp.concatenate([seg("ga"), seg("ff")], axis=1),
                          ((0, 0), (0, LANES - GLA_RANK - fox_heads))).astype(BF16)
        w_a2_pad = jnp.pad(w_gla_a2[i], ((0, LANES - GLA_RANK), (0, 0))).astype(BF16)
        w_router = jnp.pad(jnp.concatenate([w_route_group[i], w_route_expert[i]], axis=1),
                           ((0, 0), (0, LANES - N_GROUPS - N_EXPERTS))).astype(BF16)
        b_router = jnp.pad(jnp.concatenate([b_route_group[i], b_route_expert[i]]),
                           (0, LANES - N_GROUPS - N_EXPERTS))
        wog, wof, wo = w_o_gla[i].astype(BF16), w_o_fox[i].astype(BF16), w_out[i].astype(BF16)
        proj = (g_mix[i], w_main, w_k, w_v, w_small, w_a2_pad, b_gla_a[i])

        def forget(z_small, batch, seq):
            f_t = z_small[:, GLA_RANK:GLA_RANK + fox_heads].reshape(batch, seq, fox_heads).transpose(0, 2, 1)
            return fox_log_forget(f_t, b_fox_f[i])

        zp, fkp, fvp, zsp, lap = _project(hp, *proj)
        lf_t, c_t = forget(zsp, bp, sp)
        ogp, s_fin = gla_prompt(zp, lap, g_gla_out[i], bp, sp, off)
        ofp = fox_prompt(zp, fkp, fvp, c_t.reshape(bp * fox_heads, 1, sp), bp, sp, off["fq"])
        mixp = merge_branches(ogp, ofp, wog, wof, zp, off["mg"], off["mf"])
        h1p, hn_all, idp, wtp = out_proj_and_route(mixp, wo, hp, g_ffn[i], w_router, b_router, tp + ts)
        outs["kp"].append(fkp.reshape(bp, sp, fox_heads, FOX_DH))
        outs["vp"].append(fvp.reshape(bp, sp, fox_heads, FOX_DH))
        outs["lfp"].append(lf_t.transpose(0, 2, 1))
        outs["sp"].append(s_fin)

        zs, fks, fvs, zss, las = _project(hs, *proj)
        lfs_t, _ = forget(zss, 1, ts)
        lfs = lfs_t[0].T
        ogs, s_new = gla_sample(zs, las, g_gla_out[i], state_gla[i], off)
        bias = fox_sample_bias(page_table, lfs.reshape(bs, fox_heads, 1),
                               cache_fox_logf[i].transpose(0, 2, 1))
        n_pages = page_table.shape[1]
        bias = bias.transpose(0, 1, 3, 2).reshape(bs, n_pages // FOX_PAGES_PER_STEP, FOX_PAGES_PER_STEP,
                                                  page_size * fox_heads)
        heads = lambda a: a.reshape(bs, fox_heads, FOX_DH)
        ofs = fox_sample(page_table, heads(zs[:, off["fq"]:off["fq"] + d]), heads(fks), heads(fvs), bias,
                         cache_fox_k, cache_fox_v, i).reshape(bs, d)
        mixs = merge_branches(ogs, ofs, wog, wof, zs, off["mg"], off["mf"])
        h1s, hn_all, ids, wts = out_proj_and_route(mixs, wo, hs, g_ffn[i], w_router, b_router, tp + ts,
                                                   row_offset=tp, normed=hn_all)
        outs["ks"].append(fks.reshape(bs, ss, fox_heads, FOX_DH))
        outs["vs"].append(fvs.reshape(bs, ss, fox_heads, FOX_DH))
        outs["lfs"].append(lfs.reshape(bs, ss, fox_heads))
        outs["ss"].append(s_new)

        t_all = tp + ts
        n_slots = t_all * TOP_K
        t_pad = -(-t_all // 512) * 512
        ids_all = jnp.concatenate([idp, ids], axis=0)
        ids_pad = jnp.pad(ids_all, ((0, t_pad - t_all), (0, 0)), constant_values=LANES - 1)
        rank, counts = rank_slots(ids_pad)
        counts = counts[0, :N_EXPERTS].astype(I32)
        padded = -(-counts // MOE_ROWS) * MOE_ROWS
        ends = jnp.cumsum(padded)
        experts = ids_all[:, :TOP_K]
        dest = (ends - padded)[experts] + rank[:t_all, :TOP_K]
        n_blocks = -(-n_slots // MOE_ROWS) + N_EXPERTS
        n_active = (ends[-1] // MOE_ROWS).reshape(1)
        block_start = jnp.arange(n_blocks, dtype=I32) * MOE_ROWS
        block_expert = jnp.searchsorted(ends, jnp.minimum(block_start, ends[-1] - 1), side="right").astype(I32)
        tokens = jnp.broadcast_to(jnp.arange(t_all, dtype=I32)[:, None], (t_all, TOP_K))
        row_token = jnp.zeros((n_blocks * MOE_ROWS,), I32).at[dest.reshape(-1)].set(tokens.reshape(-1))
        y_rows = run_experts(block_expert, n_active.astype(I32), row_token, hn_all,
                             w_exp_gate, w_exp_up, w_exp_down, i)

        last = i == depth - 1
        assert last, "the final norm is fused into the last layer's finishing kernel"
        fin = (g_ple[i], w_ple_gate[i].astype(BF16), w_ple_proj[i].astype(BF16), g_final, y_rows)
        hp = combine_and_finish(dest[:tp].reshape(-1), h1p, wtp, p_prompt[i].reshape(tp, -1), *fin)
        hs = combine_and_finish(dest[tp:].reshape(-1), h1s, wts, p_sample[i].reshape(ts, -1), *fin)

    stack = lambda name: jnp.stack(outs[name])
    return (hp.reshape(bp, sp, d), hs.reshape(bs, ss, d),
            stack("kp"), stack("vp"), stack("lfp"), stack("sp"),
            stack("ks"), stack("vs"), stack("lfs"), stack("ss"))
```

```python
import functools

import jax
import jax.numpy as jnp
from jax import lax
from jax.experimental import pallas as pl
from jax.experimental.pallas import tpu as pltpu

F32 = jnp.float32
BF16 = jnp.bfloat16
I32 = jnp.int32
HIGHEST = lax.Precision.HIGHEST
NEG_INF = float("-inf")

RMS_EPS = 1e-6
GLA_HEADS = 4
GLA_RANK = 16
GLA_TAU = 16.0
GLA_CHUNK = 64
GLA_SUB = 16
FOX_DH = 128
N_GROUPS = 4
EXPERTS_PER_GROUP = 8
N_EXPERTS = N_GROUPS * EXPERTS_PER_GROUP
TOP_K = 2

LANES = 128
MOE_ROWS = 256
GATHER_UNROLL = 8
VMEM_LIMIT = 56 * 2**20


def _params(*sem):
    return pltpu.CompilerParams(dimension_semantics=sem, vmem_limit_bytes=VMEM_LIMIT)


def _log_sigmoid(x):
    return jnp.minimum(x, 0.0) - jnp.log1p(jnp.exp(-jnp.abs(x)))


def _rms(x, g):
    return x * lax.rsqrt(jnp.mean(x * x, axis=-1, keepdims=True) + RMS_EPS) * g


def _nt_dot(a, b):
    return lax.dot_general(a, b, (((1,), (1,)), ((), ())), preferred_element_type=F32)


def _tn_dot(a, b):
    return lax.dot_general(a, b, (((0,), (0,)), ((), ())), preferred_element_type=F32)


def _row_to_col(row):
    n = row.shape[1]
    eye = lax.broadcasted_iota(I32, (n, n), 0) == lax.broadcasted_iota(I32, (n, n), 1)
    return jnp.sum(jnp.where(eye, row, 0.0), axis=1, keepdims=True)


def _rms_cast_kernel(x_ref, g_ref, o_ref):
    o_ref[...] = _rms(x_ref[...], g_ref[...]).astype(o_ref.dtype)


def rms_cast(x, g):
    t, d = x.shape
    tm = min(512, t)
    return pl.pallas_call(
        _rms_cast_kernel,
        grid=(t // tm,),
        in_specs=[pl.BlockSpec((tm, d), lambda i: (i, 0)), pl.BlockSpec((1, d), lambda i: (0, 0))],
        out_specs=pl.BlockSpec((tm, d), lambda i: (i, 0)),
        out_shape=jax.ShapeDtypeStruct((t, d), BF16),
        compiler_params=_params("parallel"),
        name="rms_cast",
    )(x, g.reshape(1, d))


def _mm_kernel(a_ref, w_ref, o_ref):
    o_ref[...] = jnp.dot(a_ref[...], w_ref[...], preferred_element_type=F32).astype(o_ref.dtype)


def matmul(a, w, out_dtype=F32, tm=512, tn=1024):
    t, k = a.shape
    n = w.shape[1]
    tm, tn = min(tm, t), min(tn, n)
    return pl.pallas_call(
        _mm_kernel,
        grid=(n // tn, t // tm),
        in_specs=[pl.BlockSpec((tm, k), lambda j, i: (i, 0)), pl.BlockSpec((k, tn), lambda j, i: (0, j))],
        out_specs=pl.BlockSpec((tm, tn), lambda j, i: (i, j)),
        out_shape=jax.ShapeDtypeStruct((t, n), out_dtype),
        compiler_params=_params("parallel", "parallel"),
        name="matmul",
    )(a, w)


def _la_kernel(z_ref, w_ref, b_ref, o_ref):
    pre = jnp.dot(z_ref[...].astype(BF16), w_ref[...], preferred_element_type=F32) + b_ref[...]
    o_ref[...] = _log_sigmoid(pre) / GLA_TAU


def gla_log_decay(z_small, w_a2_pad, b_a):
    t = z_small.shape[0]
    n = w_a2_pad.shape[1]
    tm = min(512, t)
    return pl.pallas_call(
        _la_kernel,
        grid=(t // tm,),
        in_specs=[pl.BlockSpec((tm, LANES), lambda i: (i, 0)),
                  pl.BlockSpec((LANES, n), lambda i: (0, 0)),
                  pl.BlockSpec((1, n), lambda i: (0, 0))],
        out_specs=pl.BlockSpec((tm, n), lambda i: (i, 0)),
        out_shape=jax.ShapeDtypeStruct((t, n), F32),
        compiler_params=_params("parallel"),
        name="gla_log_decay",
    )(z_small, w_a2_pad, b_a.reshape(1, n))


def _logf_kernel(f_ref, b_ref, lf_ref, c_ref, *, blk):
    lf = _log_sigmoid(f_ref[...] + b_ref[...])
    lf_ref[...] = lf
    s = lf.shape[1]
    upper = (lax.broadcasted_iota(I32, (blk, blk), 0) <= lax.broadcasted_iota(I32, (blk, blk), 1)).astype(F32)
    carry = jnp.zeros((lf.shape[0], 1), F32)
    for j in range(s // blk):
        cs = jnp.dot(lf[:, j * blk:(j + 1) * blk], upper, precision=HIGHEST, preferred_element_type=F32) + carry
        c_ref[:, j * blk:(j + 1) * blk] = cs
        carry = cs[:, blk - 1:blk]


def fox_log_forget(f_t, b_f):
    b, h, s = f_t.shape
    blk = min(256, s)
    spec = pl.BlockSpec((None, h, s), lambda i: (i, 0, 0))
    return pl.pallas_call(
        functools.partial(_logf_kernel, blk=blk),
        grid=(b,),
        in_specs=[spec, pl.BlockSpec((h, 1), lambda i: (0, 0))],
        out_specs=[spec, spec],
        out_shape=[jax.ShapeDtypeStruct((b, h, s), F32)] * 2,
        compiler_params=_params("parallel"),
        name="fox_log_forget",
    )(f_t, b_f.reshape(h, 1))


def _gla_prompt_kernel(q_ref, k_ref, v_ref, r_ref, la_ref, g_ref, o_ref, s_ref, *, rows, heads):
    @pl.when(pl.program_id(2) == 0)
    def _():
        s_ref[...] = jnp.zeros_like(s_ref)

    c, sub = GLA_CHUNK, GLA_SUB
    nsub = c // sub
    dk = q_ref.shape[1] // heads
    dv = v_ref.shape[1] // heads
    scale = dk ** -0.5
    row = lax.broadcasted_iota(I32, (c, c), 0)
    col = lax.broadcasted_iota(I32, (c, c), 1)
    lower = (row >= col).astype(F32)
    col_sub = lax.broadcasted_iota(I32, (sub, c), 1)
    row_in_sub = lax.broadcasted_iota(I32, (c, 1), 0) % sub

    def sub_rows(x, s):
        return jnp.concatenate(
            [jnp.broadcast_to(x[i * sub + s:i * sub + s + 1, :], (sub, dk)) for i in range(nsub)], axis=0)

    def one_head(sl, hh):
        kcols = slice(hh * dk, (hh + 1) * dk)
        vcols = slice(hh * dv, (hh + 1) * dv)
        q = q_ref[sl, kcols] * scale
        k = k_ref[sl, kcols]
        vb = v_ref[sl, vcols].astype(BF16)
        b = jnp.dot(lower, la_ref[sl, kcols], precision=HIGHEST, preferred_element_type=F32)
        state = s_ref[hh]

        o = jnp.dot((q * jnp.exp(b)).astype(BF16), state.astype(BF16), preferred_element_type=F32)

        a = jnp.zeros((c, c), F32)
        for s in range(sub):
            e = jnp.exp(jnp.where(row_in_sub >= s, b - sub_rows(b, s), NEG_INF))
            w = jnp.sum(q * e * sub_rows(k, s), axis=-1, keepdims=True)
            a = jnp.where(col == (row // sub) * sub + s, w, a)
        blocks = [a[:sub]]
        for i in range(1, nsub):
            ri = slice(i * sub, (i + 1) * sub)
            edge = b[i * sub - 1:i * sub, :]
            qs = (q[ri] * jnp.exp(b[ri] - edge)).astype(BF16)
            ks = (k * jnp.exp(jnp.minimum(edge - b, 0.0))).astype(BF16)
            blocks.append(jnp.where(col_sub < i * sub, _nt_dot(qs, ks), a[ri]))
        a = jnp.concatenate(blocks, axis=0)
        o = o + jnp.dot(a.astype(BF16), vb, preferred_element_type=F32)

        r = r_ref[sl, vcols]
        o_ref[sl, vcols] = (_rms(o, g_ref[...]) * (r * jax.nn.sigmoid(r))).astype(o_ref.dtype)

        last = b[c - 1:c, :]
        kd = (k * jnp.exp(last - b)).astype(BF16)
        s_ref[hh] = _row_to_col(jnp.exp(last)) * state + _tn_dot(kd, vb)

    def chunk(ci, carry):
        sl = pl.ds(pl.multiple_of(ci * c, c), c)
        for hh in range(heads):
            one_head(sl, hh)
        return carry

    lax.fori_loop(0, rows // c, chunk, 0)


GLA_HEADS_PER_STEP = 2


def gla_prompt(z, la, g_out, batch, seq, off):
    t = z.shape[0]
    hp = GLA_HEADS_PER_STEP
    dk = la.shape[1] // GLA_HEADS
    dv = g_out.shape[0]
    wk, wv = hp * dk, hp * dv
    rows = min(512, seq)
    nt = seq // rows
    kern = functools.partial(_gla_prompt_kernel, rows=rows, heads=hp)
    rmap = lambda b, h, i: b * nt + i
    return pl.pallas_call(
        kern,
        grid=(batch, GLA_HEADS // hp, nt),
        in_specs=[
            pl.BlockSpec((rows, wk), lambda b, h, i: (rmap(b, h, i), off["gq"] // wk + h)),
            pl.BlockSpec((rows, wk), lambda b, h, i: (rmap(b, h, i), off["gk"] // wk + h)),
            pl.BlockSpec((rows, wv), lambda b, h, i: (rmap(b, h, i), off["gv"] // wv + h)),
            pl.BlockSpec((rows, wv), lambda b, h, i: (rmap(b, h, i), off["gr"] // wv + h)),
            pl.BlockSpec((rows, wk), lambda b, h, i: (rmap(b, h, i), h)),
            pl.BlockSpec((1, dv), lambda b, h, i: (0, 0)),
        ],
        out_specs=[
            pl.BlockSpec((rows, wv), lambda b, h, i: (rmap(b, h, i), h)),
            pl.BlockSpec((None, hp, dk, dv), lambda b, h, i: (b, h, 0, 0)),
        ],
        out_shape=[jax.ShapeDtypeStruct((t, GLA_HEADS * dv), BF16),
                   jax.ShapeDtypeStruct((batch, GLA_HEADS, dk, dv), F32)],
        compiler_params=_params("parallel", "parallel", "arbitrary"),
        name="gla_prompt",
    )(z, z, z, z, la, g_out.reshape(1, dv))


def _gla_sample_kernel(q_ref, k_ref, v_ref, r_ref, la_ref, g_ref, s_ref, o_ref, sn_ref):
    h, dk, dv = s_ref.shape
    for j in range(h):
        kc = slice(j * dk, (j + 1) * dk)
        vc = slice(j * dv, (j + 1) * dv)
        q = q_ref[:, kc] * dk ** -0.5
        k = k_ref[:, kc]
        v = v_ref[:, vc]
        decay = jnp.exp(la_ref[:, kc])
        state = s_ref[j]
        sn_ref[j] = _row_to_col(decay) * state + _row_to_col(k) * v
        qe = _row_to_col((q * decay).astype(BF16).astype(F32))
        o = jnp.sum(qe * state.astype(BF16).astype(F32), axis=0, keepdims=True)
        o = o + jnp.sum(q * k, axis=1, keepdims=True) * v
        r = r_ref[:, vc]
        o_ref[:, vc] = (_rms(o, g_ref[...]) * (r * jax.nn.sigmoid(r))).astype(o_ref.dtype)


def gla_sample(z, la, g_out, state, off):
    b = z.shape[0]
    _, h, dk, dv = state.shape
    wk, wv = h * dk, h * dv
    z3 = z.reshape(b, 1, z.shape[1])
    la3 = la.reshape(b, 1, la.shape[1])
    og, sn = pl.pallas_call(
        _gla_sample_kernel,
        grid=(b,),
        in_specs=[
            pl.BlockSpec((None, 1, wk), lambda i: (i, 0, off["gq"] // wk)),
            pl.BlockSpec((None, 1, wk), lambda i: (i, 0, off["gk"] // wk)),
            pl.BlockSpec((None, 1, wv), lambda i: (i, 0, off["gv"] // wv)),
            pl.BlockSpec((None, 1, wv), lambda i: (i, 0, off["gr"] // wv)),
            pl.BlockSpec((None, 1, wk), lambda i: (i, 0, 0)),
            pl.BlockSpec((1, dv), lambda i: (0, 0)),
            pl.BlockSpec((None, h, dk, dv), lambda i: (i, 0, 0, 0)),
        ],
        out_specs=[
            pl.BlockSpec((None, 1, wv), lambda i: (i, 0, 0)),
            pl.BlockSpec((None, h, dk, dv), lambda i: (i, 0, 0, 0)),
        ],
        out_shape=[jax.ShapeDtypeStruct((b, 1, wv), BF16), jax.ShapeDtypeStruct(state.shape, F32)],
        compiler_params=_params("parallel"),
        name="gla_sample",
    )(z3, z3, z3, z3, la3, g_out.reshape(1, dv), state)
    return og.reshape(b, wv), sn


def _fox_prompt_kernel(q_ref, k_ref, v_ref, c_ref, o_ref, kb_ref, vb_ref, *, blk):
    s, dh = q_ref.shape
    kb_ref[...] = k_ref[...].astype(BF16)
    vb_ref[...] = v_ref[...].astype(BF16)
    causal = lax.broadcasted_iota(I32, (blk, blk), 0) >= lax.broadcasted_iota(I32, (blk, blk), 1)
    for i in range(s // blk):
        qs = slice(i * blk, (i + 1) * blk)
        q = (q_ref[qs, :] * dh ** -0.5).astype(BF16)
        m = jnp.full((blk, 1), NEG_INF, F32)
        l = jnp.zeros((blk, 1), F32)
        acc = jnp.zeros((blk, dh), F32)
        for j in range(i + 1):
            ks = slice(j * blk, (j + 1) * blk)
            logit = _nt_dot(q, kb_ref[ks, :]) - c_ref[:, ks]
            if j == i:
                logit = jnp.where(causal, logit, NEG_INF)
            m_new = jnp.maximum(m, jnp.max(logit, axis=-1, keepdims=True))
            alpha = jnp.exp(m - m_new)
            p = jnp.exp(logit - m_new)
            l = alpha * l + jnp.sum(p, axis=-1, keepdims=True)
            acc = alpha * acc + jnp.dot(p.astype(BF16), vb_ref[ks, :], preferred_element_type=F32)
            m = m_new
        o_ref[qs, :] = (acc / l).astype(o_ref.dtype)


def fox_prompt(z, fk, fv, c_t, batch, seq, q_off):
    t, w = fk.shape
    h = w // FOX_DH
    blk = min(512, seq)
    return pl.pallas_call(
        functools.partial(_fox_prompt_kernel, blk=blk),
        grid=(batch, h),
        in_specs=[
            pl.BlockSpec((seq, FOX_DH), lambda b, j: (b, q_off // FOX_DH + j)),
            pl.BlockSpec((seq, FOX_DH), lambda b, j: (b, j)),
            pl.BlockSpec((seq, FOX_DH), lambda b, j: (b, j)),
            pl.BlockSpec((None, 1, seq), lambda b, j: (b * h + j, 0, 0)),
        ],
        out_specs=pl.BlockSpec((seq, FOX_DH), lambda b, j: (b, j)),
        out_shape=jax.ShapeDtypeStruct((t, w), BF16),
        scratch_shapes=[pltpu.VMEM((seq, FOX_DH), BF16)] * 2,
        compiler_params=_params("parallel", "parallel"),
        name="fox_prompt",
    )(z, fk, fv, c_t)


def _fox_bias_kernel(pt_ref, lfn_ref, pool_ref, o_ref, buf_ref, sem):
    b = pl.program_id(0)
    npg, h, ps = buf_ref.shape

    def page_copy(p):
        return pltpu.make_async_copy(pool_ref.at[pt_ref[b, p]], buf_ref.at[p], sem)

    for p in range(npg):
        page_copy(p).start()
    for p in range(npg):
        page_copy(p).wait()

    x = buf_ref[...]
    after = (lax.broadcasted_iota(I32, (ps, ps), 0) >= lax.broadcasted_iota(I32, (ps, ps), 1)).astype(F32)
    suffix = jnp.dot(x.reshape(npg * h, ps), after, precision=HIGHEST,
                     preferred_element_type=F32).reshape(npg, h, ps)
    run = lfn_ref[...]
    for p in range(npg - 1, -1, -1):
        o_ref[p] = suffix[p] - x[p] + run
        run = run + suffix[p][:, 0:1]


def fox_sample_bias(page_table, lf_new, pool_lf_t):
    b, npg = page_table.shape
    _, h, ps = pool_lf_t.shape
    return pl.pallas_call(
        _fox_bias_kernel,
        grid_spec=pltpu.PrefetchScalarGridSpec(
            num_scalar_prefetch=1,
            grid=(b,),
            in_specs=[pl.BlockSpec((None, h, 1), lambda i, pt: (i, 0, 0)),
                      pl.BlockSpec(memory_space=pl.ANY)],
            out_specs=pl.BlockSpec((None, npg, h, ps), lambda i, pt: (i, 0, 0, 0)),
            scratch_shapes=[pltpu.VMEM((npg, h, ps), F32), pltpu.SemaphoreType.DMA(())],
        ),
        out_shape=jax.ShapeDtypeStruct((b, npg, h, ps), F32),
        compiler_params=_params("arbitrary"),
        name="fox_sample_bias",
    )(page_table, lf_new, pool_lf_t)


def _fox_sample_kernel(pt_ref, q_ref, kn_ref, vn_ref, bias_ref, *refs, pages):
    k_refs, v_refs = refs[:pages], refs[pages:2 * pages]
    o_ref, m_ref, l_ref, acc_ref = refs[2 * pages:]
    p = pl.program_id(1)
    h, dh = q_ref.shape
    n = k_refs[0].shape[0] * h
    q = q_ref[...] * dh ** -0.5

    @pl.when(p == 0)
    def _():
        m_ref[...] = jnp.broadcast_to(jnp.sum(q * kn_ref[...], axis=1, keepdims=True), m_ref.shape)
        l_ref[...] = jnp.ones_like(l_ref)
        acc_ref[...] = vn_ref[...]

    own = lax.broadcasted_iota(I32, (h, n), 1) % h == lax.broadcasted_iota(I32, (h, n), 0)
    qb = q.astype(BF16)
    logits = [jnp.where(own, _nt_dot(qb, k_refs[r][...].reshape(n, dh).astype(BF16)) + bias_ref[r:r + 1, :], NEG_INF)
              for r in range(pages)]
    m = m_ref[:, 0:1]
    m_new = m
    for lg in logits:
        m_new = jnp.maximum(m_new, jnp.max(lg, axis=-1, keepdims=True))
    alpha = jnp.exp(m - m_new)
    l = alpha * l_ref[:, 0:1]
    acc = alpha * acc_ref[...]
    for r in range(pages):
        pr = jnp.exp(logits[r] - m_new)
        l = l + jnp.sum(pr, axis=-1, keepdims=True)
        acc = acc + jnp.dot(pr.astype(BF16), v_refs[r][...].reshape(n, dh).astype(BF16), preferred_element_type=F32)
    m_ref[...] = jnp.broadcast_to(m_new, m_ref.shape)
    l_ref[...] = jnp.broadcast_to(l, l_ref.shape)
    acc_ref[...] = acc

    @pl.when(p == pl.num_programs(1) - 1)
    def _():
        o_ref[...] = (acc / l).astype(o_ref.dtype)


FOX_PAGES_PER_STEP = 8


def fox_sample(page_table, q, k_new, v_new, bias, cache_k, cache_v, layer):
    b, npg = page_table.shape
    _, _, ps, h, dh = cache_k.shape
    g = FOX_PAGES_PER_STEP
    row = pl.BlockSpec((None, h, dh), lambda i, p, pt: (i, 0, 0))

    def page(r):
        return pl.BlockSpec((None, None, ps, h, dh), lambda i, p, pt: (layer, pt[i, p * g + r], 0, 0, 0))

    pages = [page(r) for r in range(g)]
    return pl.pallas_call(
        functools.partial(_fox_sample_kernel, pages=g),
        grid_spec=pltpu.PrefetchScalarGridSpec(
            num_scalar_prefetch=1,
            grid=(b, npg // g),
            in_specs=[row, row, row,
                      pl.BlockSpec((None, None, g, ps * h), lambda i, p, pt: (i, p, 0, 0)),
                      *pages, *pages],
            out_specs=row,
            scratch_shapes=[pltpu.VMEM((h, LANES), F32), pltpu.VMEM((h, LANES), F32), pltpu.VMEM((h, dh), F32)],
        ),
        out_shape=jax.ShapeDtypeStruct((b, h, dh), BF16),
        compiler_params=_params("parallel", "arbitrary"),
        name="fox_sample",
    )(page_table, q, k_new, v_new, bias, *([cache_k] * g), *([cache_v] * g))


def _merge_kernel(og_ref, of_ref, wg_ref, wf_ref, mg_ref, mf_ref, o_ref):
    yg = jnp.dot(og_ref[...], wg_ref[...], preferred_element_type=F32)
    yf = jnp.dot(of_ref[...], wf_ref[...], preferred_element_type=F32)
    o_ref[...] = (jax.nn.sigmoid(mg_ref[...]) * yg + jax.nn.sigmoid(mf_ref[...]) * yf).astype(o_ref.dtype)


def merge_branches(og, of, w_o_gla, w_o_fox, z, mg_off, mf_off):
    t, k = og.shape
    n = w_o_gla.shape[1]
    tm, tn = min(512, t), min(1024, n)
    return pl.pallas_call(
        _merge_kernel,
        grid=(n // tn, t // tm),
        in_specs=[pl.BlockSpec((tm, k), lambda j, i: (i, 0)),
                  pl.BlockSpec((tm, k), lambda j, i: (i, 0)),
                  pl.BlockSpec((k, tn), lambda j, i: (0, j)),
                  pl.BlockSpec((k, tn), lambda j, i: (0, j)),
                  pl.BlockSpec((tm, tn), lambda j, i: (i, mg_off // tn + j)),
                  pl.BlockSpec((tm, tn), lambda j, i: (i, mf_off // tn + j))],
        out_specs=pl.BlockSpec((tm, tn), lambda j, i: (i, j)),
        out_shape=jax.ShapeDtypeStruct((t, n), BF16),
        compiler_params=_params("parallel", "parallel"),
        name="merge_branches",
    )(og, of, w_o_gla, w_o_fox, z, z)


def _route_kernel(mix_ref, w_ref, x_ref, g_ref, wr_ref, br_ref, *refs):
    h_ref, hn_ref, id_ref, wt_ref = refs[-4:]
    h1 = x_ref[...] + jnp.dot(mix_ref[...], w_ref[...], preferred_element_type=F32)
    h_ref[...] = h1
    hn = _rms(h1, g_ref[...])
    hn_ref[...] = hn
    logits = jnp.dot(hn.astype(BF16), wr_ref[...], preferred_element_type=F32) + br_ref[...]
    lane = lax.broadcasted_iota(I32, logits.shape, 1)
    big = jnp.int32(2 ** 30)

    def top(vals):
        best = jnp.max(vals, axis=-1, keepdims=True)
        return best, jnp.min(jnp.where(vals == best, lane, big), axis=-1, keepdims=True)

    grp = jnp.where(lane < N_GROUPS, logits, NEG_INF)
    g_max, g_idx = top(grp)
    g_w = 1.0 / jnp.sum(jnp.exp(grp - g_max), axis=-1, keepdims=True)
    lo = N_GROUPS + g_idx * EXPERTS_PER_GROUP
    mine = jnp.where(lane >= lo, jnp.where(lane < lo + EXPERTS_PER_GROUP, logits, NEG_INF), NEG_INF)
    e1, i1 = top(mine)
    e2, i2 = top(jnp.where(lane == i1, NEG_INF, mine))
    z = jnp.sum(jnp.exp(mine - e1), axis=-1, keepdims=True)
    p1 = 1.0 / z
    p2 = jnp.exp(e2 - e1) / z
    w1 = g_w * p1 / (p1 + p2)
    w2 = g_w * p2 / (p1 + p2)
    id_ref[...] = jnp.where(lane == 0, i1 - N_GROUPS, jnp.where(lane == 1, i2 - N_GROUPS, 0))
    wt_ref[...] = jnp.where(lane == 0, w1, jnp.where(lane == 1, w2, 0.0))


def out_proj_and_route(mix, w_out, x, g_ffn, w_router, b_router, normed, row_offset=0):
    t, d = x.shape
    tm = min(256, t)
    assert row_offset % tm == 0
    row = lambda n: pl.BlockSpec((tm, n), lambda i: (i, 0))
    full = lambda a: pl.BlockSpec(a.shape, lambda i: (0, 0))
    g2, b2 = g_ffn.reshape(1, d), b_router.reshape(1, LANES)
    args = [mix, w_out, x, g2, w_router, b2, normed]
    in_specs = [row(d), full(w_out), row(d), full(g2), full(w_router), full(b2), pl.BlockSpec(memory_space=pl.ANY)]
    return pl.pallas_call(
        _route_kernel,
        grid=(t // tm,),
        in_specs=in_specs,
        out_specs=[row(d), pl.BlockSpec((tm, d), lambda i: (row_offset // tm + i, 0)), row(LANES), row(LANES)],
        out_shape=[jax.ShapeDtypeStruct((t, d), F32), jax.ShapeDtypeStruct(normed.shape, F32),
                   jax.ShapeDtypeStruct((t, LANES), I32), jax.ShapeDtypeStruct((t, LANES), F32)],
        input_output_aliases={6: 1},
        compiler_params=_params("parallel"),
        name="out_proj_and_route",
    )(*args)


def _rank_kernel(id_ref, rank_ref, cnt_ref):
    @pl.when(pl.program_id(0) == 0)
    def _():
        cnt_ref[...] = jnp.zeros_like(cnt_ref)

    ids = id_ref[...]
    n = ids.shape[0]
    lane = lax.broadcasted_iota(I32, ids.shape, 1)
    hit0 = lane == ids[:, 0:1]
    hit1 = lane == ids[:, 1:2]
    hits = (jnp.where(hit0, 1.0, 0.0) + jnp.where(hit1, 1.0, 0.0)).astype(BF16)
    lower = (lax.broadcasted_iota(I32, (n, n), 0) >= lax.broadcasted_iota(I32, (n, n), 1)).astype(BF16)
    seen = jnp.dot(lower, hits, preferred_element_type=F32) + cnt_ref[...]
    r0 = jnp.sum(jnp.where(hit0, seen, 0.0), axis=-1, keepdims=True) - 1.0
    r1 = jnp.sum(jnp.where(hit1, seen, 0.0), axis=-1, keepdims=True) - 1.0
    rank_ref[...] = jnp.where(lane == 0, r0, jnp.where(lane == 1, r1, 0.0)).astype(I32)
    cnt_ref[...] = seen[n - 1:n, :]


def rank_slots(ids):
    t = ids.shape[0]
    tm = 512
    return pl.pallas_call(
        _rank_kernel,
        grid=(t // tm,),
        in_specs=[pl.BlockSpec((tm, LANES), lambda i: (i, 0))],
        out_specs=[pl.BlockSpec((tm, LANES), lambda i: (i, 0)), pl.BlockSpec((1, LANES), lambda i: (0, 0))],
        out_shape=[jax.ShapeDtypeStruct((t, LANES), I32), jax.ShapeDtypeStruct((1, LANES), F32)],
        compiler_params=_params("arbitrary"),
        name="rank_slots",
    )(ids)


def _expert_kernel(be_ref, na_ref, first_ref, next_ref, grp_ref, tok_ref, x_ref, wg_hbm, wu_hbm, wd_hbm, o_ref,
                   buf_ref, sg_ref, su_ref, sd_ref, wg_ref, wu_ref, wd_ref, sem, wsem, *, layer):
    blk = pl.program_id(0)
    rows = buf_ref.shape[1]
    n_active = na_ref[0]

    @pl.when(blk == 0)
    def _():
        buf_ref[...] = jnp.zeros_like(buf_ref)

    def weight_copies(e):
        return (pltpu.make_async_copy(wg_hbm.at[layer, e], sg_ref, wsem.at[0]),
                pltpu.make_async_copy(wu_hbm.at[layer, e], su_ref, wsem.at[1]),
                pltpu.make_async_copy(wd_hbm.at[layer, e], sd_ref, wsem.at[2]))

    def row_copy(b, r):
        slot = b % 2
        tok = tok_ref[b * rows + r]
        return pltpu.make_async_copy(x_ref.at[pl.ds(tok, 1), :], buf_ref.at[slot, pl.ds(r, 1), :], sem.at[slot])

    def gather(b):
        def start(g, c):
            for j in range(GATHER_UNROLL):
                row_copy(b, g * GATHER_UNROLL + j).start()
            return c

        lax.fori_loop(0, grp_ref[b], start, 0)

    @pl.when(jnp.logical_and(blk == 0, n_active > 0))
    def _():
        for cp in weight_copies(be_ref[0]):
            cp.start(priority=1)
        gather(blk)

    @pl.when(blk + 1 < n_active)
    def _():
        gather(blk + 1)

    @pl.when(jnp.logical_and(blk < n_active, first_ref[blk] == 1))
    def _():
        for cp in weight_copies(be_ref[blk]):
            cp.wait()
        wg_ref[...] = sg_ref[...].astype(BF16)
        wu_ref[...] = su_ref[...].astype(BF16)
        wd_ref[...] = sd_ref[...].astype(BF16)

        @pl.when(next_ref[blk] >= 0)
        def _():
            for cp in weight_copies(next_ref[blk]):
                cp.start(priority=1)

    @pl.when(blk < n_active)
    def _():
        def wait(g, c):
            for j in range(GATHER_UNROLL):
                row_copy(blk, g * GATHER_UNROLL + j).wait()
            return c

        lax.fori_loop(0, grp_ref[blk], wait, 0)
        x = buf_ref[blk % 2].astype(BF16)
        gate = jnp.dot(x, wg_ref[...], preferred_element_type=F32)
        up = jnp.dot(x, wu_ref[...], preferred_element_type=F32)
        act = (gate * jax.nn.sigmoid(gate) * up).astype(BF16)
        o_ref[...] = jnp.dot(act, wd_ref[...], preferred_element_type=F32)

    @pl.when(blk >= na_ref[0])
    def _():
        o_ref[...] = jnp.zeros_like(o_ref)


def run_experts(block_expert, n_active, block_rows, row_token, x, w_gate, w_up, w_down, layer):
    n_blocks = block_expert.shape[0]
    _, _, d, ff = w_gate.shape
    rows = MOE_ROWS
    idx = jnp.arange(n_blocks, dtype=I32)
    first = jnp.concatenate([jnp.ones((1,), I32), (block_expert[1:] != block_expert[:-1]).astype(I32)])
    later = (idx[None, :] > idx[:, None]) & (block_expert[None, :] != block_expert[:, None]) & (idx[None, :] < n_active[0])
    nxt_blk = jnp.min(jnp.where(later, idx[None, :], n_blocks), axis=1)
    nxt = jnp.where(nxt_blk < n_blocks, block_expert[jnp.minimum(nxt_blk, n_blocks - 1)], -1).astype(I32)
    groups = (-(-block_rows // GATHER_UNROLL)).astype(I32)
    hbm = pl.BlockSpec(memory_space=pl.ANY)
    return pl.pallas_call(
        functools.partial(_expert_kernel, layer=layer),
        grid_spec=pltpu.PrefetchScalarGridSpec(
            num_scalar_prefetch=6,
            grid=(n_blocks,),
            in_specs=[hbm, hbm, hbm, hbm],
            out_specs=pl.BlockSpec((rows, d), lambda i, *_: (i, 0)),
            scratch_shapes=[pltpu.VMEM((2, rows, d), F32),
                            pltpu.VMEM((d, ff), F32), pltpu.VMEM((d, ff), F32), pltpu.VMEM((ff, d), F32),
                            pltpu.VMEM((d, ff), BF16), pltpu.VMEM((d, ff), BF16), pltpu.VMEM((ff, d), BF16),
                            pltpu.SemaphoreType.DMA((2,)), pltpu.SemaphoreType.DMA((3,))],
        ),
        out_shape=jax.ShapeDtypeStruct((n_blocks * rows, d), F32),
        compiler_params=_params("arbitrary"),
        name="run_experts",
    )(block_expert, n_active, first, nxt, groups, row_token, x, w_gate, w_up, w_down)


def _combine_kernel(dest_ref, h_ref, wt_ref, p_ref, gp_ref, wpg_ref, wpp_ref, gf_ref, y_ref, o_ref, buf_ref, sem):
    i = pl.program_id(0)
    tm = h_ref.shape[0]

    def row_copy(t, r, c):
        slot = t % 2
        src = dest_ref[(t * tm + r) * TOP_K + c]
        return pltpu.make_async_copy(y_ref.at[pl.ds(src, 1), :], buf_ref.at[slot, c, pl.ds(r, 1), :], sem.at[slot, c])

    def gather(t):
        def start(r, carry):
            for c in range(TOP_K):
                row_copy(t, r, c).start()
            return carry

        lax.fori_loop(0, tm, start, 0, unroll=GATHER_UNROLL)

    @pl.when(i == 0)
    def _():
        gather(i)

    @pl.when(i + 1 < pl.num_programs(0))
    def _():
        gather(i + 1)

    def wait(r, carry):
        for c in range(TOP_K):
            row_copy(i, r, c).wait()
        return carry

    lax.fori_loop(0, tm, wait, 0, unroll=GATHER_UNROLL)
    wt = wt_ref[...]
    slot = i % 2
    h2 = h_ref[...] + (wt[:, 0:1] * buf_ref[slot, 0] + wt[:, 1:2] * buf_ref[slot, 1])
    gate = jax.nn.sigmoid(jnp.dot(_rms(h2, gp_ref[...]).astype(BF16), wpg_ref[...], preferred_element_type=F32))
    ple = jnp.dot(p_ref[...].astype(BF16), wpp_ref[...], preferred_element_type=F32)
    o_ref[...] = _rms(h2 + gate * ple, gf_ref[...])


def combine_and_finish(dest, h1, wts, p, g_ple, w_ple_gate, w_ple_proj, g_final, y_rows):
    t, d = h1.shape
    tm = min(256, t)
    row = lambda n: pl.BlockSpec((tm, n), lambda i, ds: (i, 0))
    full = lambda a: pl.BlockSpec(a.shape, lambda i, ds: (0, 0))
    gp, gf = g_ple.reshape(1, d), g_final.reshape(1, d)
    return pl.pallas_call(
        _combine_kernel,
        grid_spec=pltpu.PrefetchScalarGridSpec(
            num_scalar_prefetch=1,
            grid=(t // tm,),
            in_specs=[row(d), row(LANES), row(p.shape[1]), full(gp), full(w_ple_gate), full(w_ple_proj), full(gf),
                      pl.BlockSpec(memory_space=pl.ANY)],
            out_specs=row(d),
            scratch_shapes=[pltpu.VMEM((2, TOP_K, tm, d), F32), pltpu.SemaphoreType.DMA((2, TOP_K))],
        ),
        out_shape=jax.ShapeDtypeStruct((t, d), F32),
        compiler_params=_params("arbitrary"),
        name="combine_and_finish",
    )(dest, h1, wts, p, gp, w_ple_gate, w_ple_proj, gf, y_rows)


def _segments(d):
    qk = d // 2
    sizes = dict(gq=qk, gk=qk, gv=d, gr=d, ga=GLA_RANK, fq=d, fk=d, fv=d, ff=d // FOX_DH, mg=d, mf=d)
    start, acc = {}, 0
    for name, size in sizes.items():
        start[name] = acc
        acc += size
    return sizes, start


def _project(x, g_mix, w_main, w_k, w_v, w_small, w_a2_pad, b_a):
    u = rms_cast(x, g_mix)
    return (matmul(u, w_main), matmul(u, w_k), matmul(u, w_v),
            *_small(matmul(u, w_small, tn=LANES), w_a2_pad, b_a))


def _small(z_small, w_a2_pad, b_a):
    return z_small, gla_log_decay(z_small, w_a2_pad, b_a)


def kernel(x_prompt, x_sample, cache_fox_k, cache_fox_v, cache_fox_logf, state_gla, page_table,
           p_prompt, p_sample, g_mix, w_in, w_gla_a2, b_gla_a, b_fox_f, g_gla_out, w_o_gla, w_o_fox, w_out,
           g_ffn, w_route_group, b_route_group, w_route_expert, b_route_expert, w_exp_gate, w_exp_up,
           w_exp_down, g_ple, w_ple_gate, w_ple_proj, g_final):
    depth = w_in.shape[0]
    bp, sp, d = x_prompt.shape
    bs, ss, _ = x_sample.shape
    assert ss == 1, "the sample group carries one new token per sequence"
    fox_heads = d // FOX_DH
    n_phys, page_size = cache_fox_k.shape[1:3]
    tp, ts = bp * sp, bs * ss
    sizes, start = _segments(d)
    main = ("gq", "gk", "gv", "gr", "fq", "mg", "mf")
    off, acc = {}, 0
    for name in main:
        off[name] = acc
        acc += sizes[name]

    hp = x_prompt.reshape(tp, d)
    hs = x_sample.reshape(ts, d)
    outs = {name: [] for name in ("kp", "vp", "lfp", "sp", "ks", "vs", "lfs", "ss")}
    for i in range(depth):
        w = w_in[i]
        seg = lambda name: w[:, start[name]:start[name] + sizes[name]]
        w_main = jnp.concatenate([seg(n) for n in main], axis=1).astype(BF16)
        w_k, w_v = seg("fk").astype(BF16), seg("fv").astype(BF16)
        w_small = jnp.pad(jnp.concatenate([seg("ga"), seg("ff")], axis=1),
                          ((0, 0), (0, LANES - GLA_RANK - fox_heads))).astype(BF16)
        w_a2_pad = jnp.pad(w_gla_a2[i], ((0, LANES - GLA_RANK), (0, 0))).astype(BF16)
        w_router = jnp.pad(jnp.concatenate([w_route_group[i], w_route_expert[i]], axis=1),
                           ((0, 0), (0, LANES - N_GROUPS - N_EXPERTS))).astype(BF16)
        b_router = jnp.pad(jnp.concatenate([b_route_group[i], b_route_expert[i]]),
                           (0, LANES - N_GROUPS - N_EXPERTS))
        wog, wof, wo = w_o_gla[i].astype(BF16), w_o_fox[i].astype(BF16), w_out[i].astype(BF16)
        proj = (g_mix[i], w_main, w_k, w_v, w_small, w_a2_pad, b_gla_a[i])

        def forget(z_small, batch, seq):
            f_t = z_small[:, GLA_RANK:GLA_RANK + fox_heads].reshape(batch, seq, fox_heads).transpose(0, 2, 1)
            return fox_log_forget(f_t, b_fox_f[i])

        zp, fkp, fvp, zsp, lap = _project(hp, *proj)
        lf_t, c_t = forget(zsp, bp, sp)
        ogp, s_fin = gla_prompt(zp, lap, g_gla_out[i], bp, sp, off)
        ofp = fox_prompt(zp, fkp, fvp, c_t.reshape(bp * fox_heads, 1, sp), bp, sp, off["fq"])
        mixp = merge_branches(ogp, ofp, wog, wof, zp, off["mg"], off["mf"])
        h1p, hn_all, idp, wtp = out_proj_and_route(mixp, wo, hp, g_ffn[i], w_router, b_router,
                                                   jnp.zeros((tp + ts, d), F32))
        outs["kp"].append(fkp.reshape(bp, sp, fox_heads, FOX_DH))
        outs["vp"].append(fvp.reshape(bp, sp, fox_heads, FOX_DH))
        outs["lfp"].append(lf_t.transpose(0, 2, 1))
        outs["sp"].append(s_fin)

        zs, fks, fvs, zss, las = _project(hs, *proj)
        lfs_t, _ = forget(zss, 1, ts)
        lfs = lfs_t[0].T
        ogs, s_new = gla_sample(zs, las, g_gla_out[i], state_gla[i], off)
        bias = fox_sample_bias(page_table, lfs.reshape(bs, fox_heads, 1),
                               cache_fox_logf[i].transpose(0, 2, 1))
        n_pages = page_table.shape[1]
        bias = bias.transpose(0, 1, 3, 2).reshape(bs, n_pages // FOX_PAGES_PER_STEP, FOX_PAGES_PER_STEP,
                                                  page_size * fox_heads)
        heads = lambda a: a.reshape(bs, fox_heads, FOX_DH)
        ofs = fox_sample(page_table, heads(zs[:, off["fq"]:off["fq"] + d]), heads(fks), heads(fvs), bias,
                         cache_fox_k, cache_fox_v, i).reshape(bs, d)
        mixs = merge_branches(ogs, ofs, wog, wof, zs, off["mg"], off["mf"])
        h1s, hn_all, ids, wts = out_proj_and_route(mixs, wo, hs, g_ffn[i], w_router, b_router, hn_all,
                                                   row_offset=tp)
        outs["ks"].append(fks.reshape(bs, ss, fox_heads, FOX_DH))
        outs["vs"].append(fvs.reshape(bs, ss, fox_heads, FOX_DH))
        outs["lfs"].append(lfs.reshape(bs, ss, fox_heads))
        outs["ss"].append(s_new)

        t_all = tp + ts
        n_slots = t_all * TOP_K
        t_pad = -(-t_all // 512) * 512
        ids_all = jnp.concatenate([idp, ids], axis=0)
        ids_pad = jnp.pad(ids_all, ((0, t_pad - t_all), (0, 0)), constant_values=LANES - 1)
        rank, counts = rank_slots(ids_pad)
        counts = counts[0, :N_EXPERTS].astype(I32)
        padded = -(-counts // MOE_ROWS) * MOE_ROWS
        ends = jnp.cumsum(padded)
        experts = ids_all[:, :TOP_K]
        dest = (ends - padded)[experts] + rank[:t_all, :TOP_K]
        n_blocks = -(-n_slots // MOE_ROWS) + N_EXPERTS
        n_active = (ends[-1] // MOE_ROWS).reshape(1)
        block_start = jnp.arange(n_blocks, dtype=I32) * MOE_ROWS
        block_expert = jnp.searchsorted(ends, jnp.minimum(block_start, ends[-1] - 1), side="right").astype(I32)
        tokens = jnp.broadcast_to(jnp.arange(t_all, dtype=I32)[:, None], (t_all, TOP_K))
        row_token = jnp.zeros((n_blocks * MOE_ROWS,), I32).at[dest.reshape(-1)].set(tokens.reshape(-1))
        block_rows = jnp.clip(counts[block_expert] - (block_start - (ends - padded)[block_expert]), 0, MOE_ROWS)
        y_rows = run_experts(block_expert, n_active.astype(I32), block_rows, row_token, hn_all,
                             w_exp_gate, w_exp_up, w_exp_down, i)

        last = i == depth - 1
        assert last, "the final norm is fused into the last layer's finishing kernel"
        fin = (g_ple[i], w_ple_gate[i].astype(BF16), w_ple_proj[i].astype(BF16), g_final, y_rows)
        hp = combine_and_finish(dest[:tp].reshape(-1), h1p, wtp, p_prompt[i].reshape(tp, -1), *fin)
        hs = combine_and_finish(dest[tp:].reshape(-1), h1s, wts, p_sample[i].reshape(ts, -1), *fin)

    stack = lambda name: jnp.stack(outs[name])
    return (hp.reshape(bp, sp, d), hs.reshape(bs, ss, d),
            stack("kp"), stack("vp"), stack("lfp"), stack("sp"),
            stack("ks"), stack("vs"), stack("lfs"), stack("ss"))
```

```python
import functools

import jax
import jax.numpy as jnp
from jax import lax
from jax.experimental import pallas as pl
from jax.experimental.pallas import tpu as pltpu

F32 = jnp.float32
BF16 = jnp.bfloat16
I32 = jnp.int32
HIGHEST = lax.Precision.HIGHEST
NEG_INF = float("-inf")

RMS_EPS = 1e-6
GLA_HEADS = 4
GLA_RANK = 16
GLA_TAU = 16.0
GLA_CHUNK = 64
GLA_SUB = 16
FOX_DH = 128
N_GROUPS = 4
EXPERTS_PER_GROUP = 8
N_EXPERTS = N_GROUPS * EXPERTS_PER_GROUP
TOP_K = 2

LANES = 128
MOE_ROWS = 256
GATHER_UNROLL = 8
VMEM_LIMIT = 56 * 2**20


def _params(*sem):
    return pltpu.CompilerParams(dimension_semantics=sem, vmem_limit_bytes=VMEM_LIMIT)


def _log_sigmoid(x):
    return jnp.minimum(x, 0.0) - jnp.log1p(jnp.exp(-jnp.abs(x)))


def _rms(x, g):
    return x * lax.rsqrt(jnp.mean(x * x, axis=-1, keepdims=True) + RMS_EPS) * g


def _nt_dot(a, b):
    return lax.dot_general(a, b, (((1,), (1,)), ((), ())), preferred_element_type=F32)


def _tn_dot(a, b):
    return lax.dot_general(a, b, (((0,), (0,)), ((), ())), preferred_element_type=F32)


def _row_to_col(row):
    n = row.shape[1]
    eye = lax.broadcasted_iota(I32, (n, n), 0) == lax.broadcasted_iota(I32, (n, n), 1)
    return jnp.sum(jnp.where(eye, row, 0.0), axis=1, keepdims=True)


def _rms_cast_kernel(x_ref, g_ref, o_ref):
    o_ref[...] = _rms(x_ref[...], g_ref[...]).astype(o_ref.dtype)


def rms_cast(x, g):
    t, d = x.shape
    tm = min(512, t)
    return pl.pallas_call(
        _rms_cast_kernel,
        grid=(t // tm,),
        in_specs=[pl.BlockSpec((tm, d), lambda i: (i, 0)), pl.BlockSpec((1, d), lambda i: (0, 0))],
        out_specs=pl.BlockSpec((tm, d), lambda i: (i, 0)),
        out_shape=jax.ShapeDtypeStruct((t, d), BF16),
        compiler_params=_params("parallel"),
        name="rms_cast",
    )(x, g.reshape(1, d))


def _mm_kernel(a_ref, w_ref, o_ref):
    o_ref[...] = jnp.dot(a_ref[...], w_ref[...], preferred_element_type=F32).astype(o_ref.dtype)


def matmul(a, w, out_dtype=F32, tm=512, tn=1024):
    t, k = a.shape
    n = w.shape[1]
    tm, tn = min(tm, t), min(tn, n)
    return pl.pallas_call(
        _mm_kernel,
        grid=(n // tn, t // tm),
        in_specs=[pl.BlockSpec((tm, k), lambda j, i: (i, 0)), pl.BlockSpec((k, tn), lambda j, i: (0, j))],
        out_specs=pl.BlockSpec((tm, tn), lambda j, i: (i, j)),
        out_shape=jax.ShapeDtypeStruct((t, n), out_dtype),
        compiler_params=_params("parallel", "parallel"),
        name="matmul",
    )(a, w)


def _la_kernel(z_ref, w_ref, b_ref, o_ref):
    pre = jnp.dot(z_ref[...].astype(BF16), w_ref[...], preferred_element_type=F32) + b_ref[...]
    o_ref[...] = _log_sigmoid(pre) / GLA_TAU


def gla_log_decay(z_small, w_a2_pad, b_a):
    t = z_small.shape[0]
    n = w_a2_pad.shape[1]
    tm = min(512, t)
    return pl.pallas_call(
        _la_kernel,
        grid=(t // tm,),
        in_specs=[pl.BlockSpec((tm, LANES), lambda i: (i, 0)),
                  pl.BlockSpec((LANES, n), lambda i: (0, 0)),
                  pl.BlockSpec((1, n), lambda i: (0, 0))],
        out_specs=pl.BlockSpec((tm, n), lambda i: (i, 0)),
        out_shape=jax.ShapeDtypeStruct((t, n), F32),
        compiler_params=_params("parallel"),
        name="gla_log_decay",
    )(z_small, w_a2_pad, b_a.reshape(1, n))


def _logf_kernel(f_ref, b_ref, lf_ref, c_ref, *, blk):
    lf = _log_sigmoid(f_ref[...] + b_ref[...])
    lf_ref[...] = lf
    s = lf.shape[1]
    upper = (lax.broadcasted_iota(I32, (blk, blk), 0) <= lax.broadcasted_iota(I32, (blk, blk), 1)).astype(F32)
    carry = jnp.zeros((lf.shape[0], 1), F32)
    for j in range(s // blk):
        cs = jnp.dot(lf[:, j * blk:(j + 1) * blk], upper, precision=HIGHEST, preferred_element_type=F32) + carry
        c_ref[:, j * blk:(j + 1) * blk] = cs
        carry = cs[:, blk - 1:blk]


def fox_log_forget(f_t, b_f):
    b, h, s = f_t.shape
    blk = min(256, s)
    spec = pl.BlockSpec((None, h, s), lambda i: (i, 0, 0))
    return pl.pallas_call(
        functools.partial(_logf_kernel, blk=blk),
        grid=(b,),
        in_specs=[spec, pl.BlockSpec((h, 1), lambda i: (0, 0))],
        out_specs=[spec, spec],
        out_shape=[jax.ShapeDtypeStruct((b, h, s), F32)] * 2,
        compiler_params=_params("parallel"),
        name="fox_log_forget",
    )(f_t, b_f.reshape(h, 1))


def _gla_prompt_kernel(q_ref, k_ref, v_ref, r_ref, la_ref, g_ref, o_ref, s_ref, *, rows, heads):
    @pl.when(pl.program_id(2) == 0)
    def _():
        s_ref[...] = jnp.zeros_like(s_ref)

    c, sub = GLA_CHUNK, GLA_SUB
    nsub = c // sub
    dk = q_ref.shape[1] // heads
    dv = v_ref.shape[1] // heads
    scale = dk ** -0.5
    row = lax.broadcasted_iota(I32, (c, c), 0)
    col = lax.broadcasted_iota(I32, (c, c), 1)
    lower = (row >= col).astype(F32)
    col_sub = lax.broadcasted_iota(I32, (sub, c), 1)
    row_in_sub = lax.broadcasted_iota(I32, (c, 1), 0) % sub

    def sub_rows(x, s):
        return jnp.concatenate(
            [jnp.broadcast_to(x[i * sub + s:i * sub + s + 1, :], (sub, dk)) for i in range(nsub)], axis=0)

    def one_head(sl, hh):
        kcols = slice(hh * dk, (hh + 1) * dk)
        vcols = slice(hh * dv, (hh + 1) * dv)
        q = q_ref[sl, kcols] * scale
        k = k_ref[sl, kcols]
        vb = v_ref[sl, vcols].astype(BF16)
        b = jnp.dot(lower, la_ref[sl, kcols], precision=HIGHEST, preferred_element_type=F32)
        state = s_ref[hh]

        o = jnp.dot((q * jnp.exp(b)).astype(BF16), state.astype(BF16), preferred_element_type=F32)

        a = jnp.zeros((c, c), F32)
        for s in range(sub):
            e = jnp.exp(jnp.where(row_in_sub >= s, b - sub_rows(b, s), NEG_INF))
            w = jnp.sum(q * e * sub_rows(k, s), axis=-1, keepdims=True)
            a = jnp.where(col == (row // sub) * sub + s, w, a)
        blocks = [a[:sub]]
        for i in range(1, nsub):
            ri = slice(i * sub, (i + 1) * sub)
            edge = b[i * sub - 1:i * sub, :]
            qs = (q[ri] * jnp.exp(b[ri] - edge)).astype(BF16)
            ks = (k * jnp.exp(jnp.minimum(edge - b, 0.0))).astype(BF16)
            blocks.append(jnp.where(col_sub < i * sub, _nt_dot(qs, ks), a[ri]))
        a = jnp.concatenate(blocks, axis=0)
        o = o + jnp.dot(a.astype(BF16), vb, preferred_element_type=F32)

        r = r_ref[sl, vcols]
        o_ref[sl, vcols] = (_rms(o, g_ref[...]) * (r * jax.nn.sigmoid(r))).astype(o_ref.dtype)

        last = b[c - 1:c, :]
        kd = (k * jnp.exp(last - b)).astype(BF16)
        s_ref[hh] = _row_to_col(jnp.exp(last)) * state + _tn_dot(kd, vb)

    def chunk(ci, carry):
        sl = pl.ds(pl.multiple_of(ci * c, c), c)
        for hh in range(heads):
            one_head(sl, hh)
        return carry

    lax.fori_loop(0, rows // c, chunk, 0)


GLA_HEADS_PER_STEP = 2


def gla_prompt(z, la, g_out, batch, seq, off):
    t = z.shape[0]
    hp = GLA_HEADS_PER_STEP
    dk = la.shape[1] // GLA_HEADS
    dv = g_out.shape[0]
    wk, wv = hp * dk, hp * dv
    rows = min(512, seq)
    nt = seq // rows
    kern = functools.partial(_gla_prompt_kernel, rows=rows, heads=hp)
    rmap = lambda b, h, i: b * nt + i
    return pl.pallas_call(
        kern,
        grid=(batch, GLA_HEADS // hp, nt),
        in_specs=[
            pl.BlockSpec((rows, wk), lambda b, h, i: (rmap(b, h, i), off["gq"] // wk + h)),
            pl.BlockSpec((rows, wk), lambda b, h, i: (rmap(b, h, i), off["gk"] // wk + h)),
            pl.BlockSpec((rows, wv), lambda b, h, i: (rmap(b, h, i), off["gv"] // wv + h)),
            pl.BlockSpec((rows, wv), lambda b, h, i: (rmap(b, h, i), off["gr"] // wv + h)),
            pl.BlockSpec((rows, wk), lambda b, h, i: (rmap(b, h, i), h)),
            pl.BlockSpec((1, dv), lambda b, h, i: (0, 0)),
        ],
        out_specs=[
            pl.BlockSpec((rows, wv), lambda b, h, i: (rmap(b, h, i), h)),
            pl.BlockSpec((None, hp, dk, dv), lambda b, h, i: (b, h, 0, 0)),
        ],
        out_shape=[jax.ShapeDtypeStruct((t, GLA_HEADS * dv), BF16),
                   jax.ShapeDtypeStruct((batch, GLA_HEADS, dk, dv), F32)],
        compiler_params=_params("parallel", "parallel", "arbitrary"),
        name="gla_prompt",
    )(z, z, z, z, la, g_out.reshape(1, dv))


def _gla_sample_kernel(q_ref, k_ref, v_ref, r_ref, la_ref, g_ref, s_ref, o_ref, sn_ref):
    h, dk, dv = s_ref.shape
    for j in range(h):
        kc = slice(j * dk, (j + 1) * dk)
        vc = slice(j * dv, (j + 1) * dv)
        q = q_ref[:, kc] * dk ** -0.5
        k = k_ref[:, kc]
        v = v_ref[:, vc]
        decay = jnp.exp(la_ref[:, kc])
        state = s_ref[j]
        sn_ref[j] = _row_to_col(decay) * state + _row_to_col(k) * v
        qe = _row_to_col((q * decay).astype(BF16).astype(F32))
        o = jnp.sum(qe * state.astype(BF16).astype(F32), axis=0, keepdims=True)
        o = o + jnp.sum(q * k, axis=1, keepdims=True) * v
        r = r_ref[:, vc]
        o_ref[:, vc] = (_rms(o, g_ref[...]) * (r * jax.nn.sigmoid(r))).astype(o_ref.dtype)


def gla_sample(z, la, g_out, state, off):
    b = z.shape[0]
    _, h, dk, dv = state.shape
    wk, wv = h * dk, h * dv
    z3 = z.reshape(b, 1, z.shape[1])
    la3 = la.reshape(b, 1, la.shape[1])
    og, sn = pl.pallas_call(
        _gla_sample_kernel,
        grid=(b,),
        in_specs=[
            pl.BlockSpec((None, 1, wk), lambda i: (i, 0, off["gq"] // wk)),
            pl.BlockSpec((None, 1, wk), lambda i: (i, 0, off["gk"] // wk)),
            pl.BlockSpec((None, 1, wv), lambda i: (i, 0, off["gv"] // wv)),
            pl.BlockSpec((None, 1, wv), lambda i: (i, 0, off["gr"] // wv)),
            pl.BlockSpec((None, 1, wk), lambda i: (i, 0, 0)),
            pl.BlockSpec((1, dv), lambda i: (0, 0)),
            pl.BlockSpec((None, h, dk, dv), lambda i: (i, 0, 0, 0)),
        ],
        out_specs=[
            pl.BlockSpec((None, 1, wv), lambda i: (i, 0, 0)),
            pl.BlockSpec((None, h, dk, dv), lambda i: (i, 0, 0, 0)),
        ],
        out_shape=[jax.ShapeDtypeStruct((b, 1, wv), BF16), jax.ShapeDtypeStruct(state.shape, F32)],
        compiler_params=_params("parallel"),
        name="gla_sample",
    )(z3, z3, z3, z3, la3, g_out.reshape(1, dv), state)
    return og.reshape(b, wv), sn


def _fox_prompt_kernel(q_ref, k_ref, v_ref, c_ref, o_ref, ko_ref, vo_ref, kb_ref, vb_ref, *, blk):
    s, dh = q_ref.shape
    ko_ref[...] = k_ref[...]
    vo_ref[...] = v_ref[...]
    kb_ref[...] = k_ref[...].astype(BF16)
    vb_ref[...] = v_ref[...].astype(BF16)
    causal = lax.broadcasted_iota(I32, (blk, blk), 0) >= lax.broadcasted_iota(I32, (blk, blk), 1)
    for i in range(s // blk):
        qs = slice(i * blk, (i + 1) * blk)
        q = (q_ref[qs, :] * dh ** -0.5).astype(BF16)
        m = jnp.full((blk, 1), NEG_INF, F32)
        l = jnp.zeros((blk, 1), F32)
        acc = jnp.zeros((blk, dh), F32)
        for j in range(i + 1):
            ks = slice(j * blk, (j + 1) * blk)
            logit = _nt_dot(q, kb_ref[ks, :]) - c_ref[:, ks]
            if j == i:
                logit = jnp.where(causal, logit, NEG_INF)
            m_new = jnp.maximum(m, jnp.max(logit, axis=-1, keepdims=True))
            alpha = jnp.exp(m - m_new)
            p = jnp.exp(logit - m_new)
            l = alpha * l + jnp.sum(p, axis=-1, keepdims=True)
            acc = alpha * acc + jnp.dot(p.astype(BF16), vb_ref[ks, :], preferred_element_type=F32)
            m = m_new
        o_ref[qs, :] = (acc / l).astype(o_ref.dtype)


def fox_prompt(z, c_t, batch, seq, off, h):
    t = z.shape[0]
    w = h * FOX_DH
    blk = min(512, seq)
    head = pl.BlockSpec((seq, FOX_DH), lambda b, j: (b, j))
    return pl.pallas_call(
        functools.partial(_fox_prompt_kernel, blk=blk),
        grid=(batch, h),
        in_specs=[
            pl.BlockSpec((seq, FOX_DH), lambda b, j: (b, off["fq"] // FOX_DH + j)),
            pl.BlockSpec((seq, FOX_DH), lambda b, j: (b, off["fk"] // FOX_DH + j)),
            pl.BlockSpec((seq, FOX_DH), lambda b, j: (b, off["fv"] // FOX_DH + j)),
            pl.BlockSpec((None, 1, seq), lambda b, j: (b * h + j, 0, 0)),
        ],
        out_specs=[head, head, head],
        out_shape=[jax.ShapeDtypeStruct((t, w), BF16), jax.ShapeDtypeStruct((t, w), F32),
                   jax.ShapeDtypeStruct((t, w), F32)],
        scratch_shapes=[pltpu.VMEM((seq, FOX_DH), BF16)] * 2,
        compiler_params=_params("parallel", "parallel"),
        name="fox_prompt",
    )(z, z, z, c_t)


def _fox_bias_kernel(pt_ref, lfn_ref, pool_ref, o_ref, buf_ref, sem):
    b = pl.program_id(0)
    npg, h, ps = buf_ref.shape

    def page_copy(p):
        return pltpu.make_async_copy(pool_ref.at[pt_ref[b, p]], buf_ref.at[p], sem)

    for p in range(npg):
        page_copy(p).start()
    for p in range(npg):
        page_copy(p).wait()

    x = buf_ref[...]
    after = (lax.broadcasted_iota(I32, (ps, ps), 0) >= lax.broadcasted_iota(I32, (ps, ps), 1)).astype(F32)
    suffix = jnp.dot(x.reshape(npg * h, ps), after, precision=HIGHEST,
                     preferred_element_type=F32).reshape(npg, h, ps)
    run = lfn_ref[...]
    for p in range(npg - 1, -1, -1):
        o_ref[p] = suffix[p] - x[p] + run
        run = run + suffix[p][:, 0:1]


def fox_sample_bias(page_table, lf_new, pool_lf_t):
    b, npg = page_table.shape
    _, h, ps = pool_lf_t.shape
    return pl.pallas_call(
        _fox_bias_kernel,
        grid_spec=pltpu.PrefetchScalarGridSpec(
            num_scalar_prefetch=1,
            grid=(b,),
            in_specs=[pl.BlockSpec((None, h, 1), lambda i, pt: (i, 0, 0)),
                      pl.BlockSpec(memory_space=pl.ANY)],
            out_specs=pl.BlockSpec((None, npg, h, ps), lambda i, pt: (i, 0, 0, 0)),
            scratch_shapes=[pltpu.VMEM((npg, h, ps), F32), pltpu.SemaphoreType.DMA(())],
        ),
        out_shape=jax.ShapeDtypeStruct((b, npg, h, ps), F32),
        compiler_params=_params("arbitrary"),
        name="fox_sample_bias",
    )(page_table, lf_new, pool_lf_t)


def _fox_sample_kernel(pt_ref, q_ref, kn_ref, vn_ref, bias_ref, *refs, pages, with_matmul):
    k_refs, v_refs = refs[:pages], refs[pages:2 * pages]
    if with_matmul:
        a_ref, w_ref, o_ref, z_ref, m_ref, l_ref, acc_ref = refs[2 * pages:]
        z_ref[...] = jnp.dot(a_ref[...], w_ref[...], preferred_element_type=F32)
    else:
        o_ref, m_ref, l_ref, acc_ref = refs[2 * pages:]
    p = pl.program_id(1)
    h, dh = q_ref.shape
    n = k_refs[0].shape[0] * h
    q = q_ref[...] * dh ** -0.5

    @pl.when(p == 0)
    def _():
        m_ref[...] = jnp.broadcast_to(jnp.sum(q * kn_ref[...], axis=1, keepdims=True), m_ref.shape)
        l_ref[...] = jnp.ones_like(l_ref)
        acc_ref[...] = vn_ref[...]

    own = lax.broadcasted_iota(I32, (h, n), 1) % h == lax.broadcasted_iota(I32, (h, n), 0)
    qb = q.astype(BF16)
    logits = [jnp.where(own, _nt_dot(qb, k_refs[r][...].reshape(n, dh).astype(BF16)) + bias_ref[r:r + 1, :], NEG_INF)
              for r in range(pages)]
    m = m_ref[:, 0:1]
    m_new = m
    for lg in logits:
        m_new = jnp.maximum(m_new, jnp.max(lg, axis=-1, keepdims=True))
    alpha = jnp.exp(m - m_new)
    l = alpha * l_ref[:, 0:1]
    acc = alpha * acc_ref[...]
    for r in range(pages):
        pr = jnp.exp(logits[r] - m_new)
        l = l + jnp.sum(pr, axis=-1, keepdims=True)
        acc = acc + jnp.dot(pr.astype(BF16), v_refs[r][...].reshape(n, dh).astype(BF16), preferred_element_type=F32)
    m_ref[...] = jnp.broadcast_to(m_new, m_ref.shape)
    l_ref[...] = jnp.broadcast_to(l, l_ref.shape)
    acc_ref[...] = acc

    @pl.when(p == pl.num_programs(1) - 1)
    def _():
        o_ref[...] = (acc / l).astype(o_ref.dtype)


FOX_PAGES_PER_STEP = 8


PROJ_TILE = (512, 1024)


def projection_fits_decode(page_table, a, w):
    tm, tn = PROJ_TILE
    steps = page_table.shape[0] * (page_table.shape[1] // FOX_PAGES_PER_STEP)
    return a.shape[0] % tm == 0 and w.shape[1] % tn == 0 and (a.shape[0] // tm) * (w.shape[1] // tn) == steps


def fox_sample(page_table, q, k_new, v_new, bias, cache_k, cache_v, layer, proj=None):
    b, npg = page_table.shape
    _, _, ps, h, dh = cache_k.shape
    g = FOX_PAGES_PER_STEP
    row = pl.BlockSpec((None, h, dh), lambda i, p, pt: (i, 0, 0))
    extra_in, extra_args, out_specs = [], [], row
    out_shape = jax.ShapeDtypeStruct((b, h, dh), BF16)
    if proj is not None:
        a, w = proj
        tm, tn = PROJ_TILE
        k = a.shape[1]
        nr = a.shape[0] // tm
        tile = lambda i, p: i * (npg // g) + p
        extra_in = [pl.BlockSpec((tm, k), lambda i, p, pt: (tile(i, p) % nr, 0)),
                    pl.BlockSpec((k, tn), lambda i, p, pt: (0, tile(i, p) // nr))]
        extra_args = [a, w]
        out_specs = [row, pl.BlockSpec((tm, tn), lambda i, p, pt: (tile(i, p) % nr, tile(i, p) // nr))]
        out_shape = [out_shape, jax.ShapeDtypeStruct((a.shape[0], w.shape[1]), F32)]

    def page(r):
        return pl.BlockSpec((None, None, ps, h, dh), lambda i, p, pt: (layer, pt[i, p * g + r], 0, 0, 0))

    pages = [page(r) for r in range(g)]
    return pl.pallas_call(
        functools.partial(_fox_sample_kernel, pages=g, with_matmul=proj is not None),
        grid_spec=pltpu.PrefetchScalarGridSpec(
            num_scalar_prefetch=1,
            grid=(b, npg // g),
            in_specs=[row, row, row,
                      pl.BlockSpec((None, None, g, ps * h), lambda i, p, pt: (i, p, 0, 0)),
                      *pages, *pages, *extra_in],
            out_specs=out_specs,
            scratch_shapes=[pltpu.VMEM((h, LANES), F32), pltpu.VMEM((h, LANES), F32), pltpu.VMEM((h, dh), F32)],
        ),
        out_shape=out_shape,
        compiler_params=_params("arbitrary", "arbitrary"),
        name="fox_sample",
    )(page_table, q, k_new, v_new, bias, *([cache_k] * g), *([cache_v] * g), *extra_args)


def _merge_kernel(og_ref, of_ref, wg_ref, wf_ref, mg_ref, mf_ref, o_ref):
    yg = jnp.dot(og_ref[...], wg_ref[...], preferred_element_type=F32)
    yf = jnp.dot(of_ref[...], wf_ref[...], preferred_element_type=F32)
    o_ref[...] = (jax.nn.sigmoid(mg_ref[...]) * yg + jax.nn.sigmoid(mf_ref[...]) * yf).astype(o_ref.dtype)


def merge_branches(og, of, w_o_gla, w_o_fox, z, mg_off, mf_off):
    t, k = og.shape
    n = w_o_gla.shape[1]
    tm, tn = min(512, t), min(1024, n)
    return pl.pallas_call(
        _merge_kernel,
        grid=(n // tn, t // tm),
        in_specs=[pl.BlockSpec((tm, k), lambda j, i: (i, 0)),
                  pl.BlockSpec((tm, k), lambda j, i: (i, 0)),
                  pl.BlockSpec((k, tn), lambda j, i: (0, j)),
                  pl.BlockSpec((k, tn), lambda j, i: (0, j)),
                  pl.BlockSpec((tm, tn), lambda j, i: (i, mg_off // tn + j)),
                  pl.BlockSpec((tm, tn), lambda j, i: (i, mf_off // tn + j))],
        out_specs=pl.BlockSpec((tm, tn), lambda j, i: (i, j)),
        out_shape=jax.ShapeDtypeStruct((t, n), BF16),
        compiler_params=_params("parallel", "parallel"),
        name="merge_branches",
    )(og, of, w_o_gla, w_o_fox, z, z)


def _route_kernel(mix_ref, w_ref, x_ref, g_ref, wr_ref, br_ref, *refs):
    h_ref, hn_ref, id_ref, wt_ref = refs[-4:]
    h1 = x_ref[...] + jnp.dot(mix_ref[...], w_ref[...], preferred_element_type=F32)
    h_ref[...] = h1
    hn = _rms(h1, g_ref[...])
    hn_ref[...] = hn
    logits = jnp.dot(hn.astype(BF16), wr_ref[...], preferred_element_type=F32) + br_ref[...]
    lane = lax.broadcasted_iota(I32, logits.shape, 1)
    big = jnp.int32(2 ** 30)

    def top(vals):
        best = jnp.max(vals, axis=-1, keepdims=True)
        return best, jnp.min(jnp.where(vals == best, lane, big), axis=-1, keepdims=True)

    grp = jnp.where(lane < N_GROUPS, logits, NEG_INF)
    g_max, g_idx = top(grp)
    g_w = 1.0 / jnp.sum(jnp.exp(grp - g_max), axis=-1, keepdims=True)
    lo = N_GROUPS + g_idx * EXPERTS_PER_GROUP
    mine = jnp.where(lane >= lo, jnp.where(lane < lo + EXPERTS_PER_GROUP, logits, NEG_INF), NEG_INF)
    e1, i1 = top(mine)
    e2, i2 = top(jnp.where(lane == i1, NEG_INF, mine))
    z = jnp.sum(jnp.exp(mine - e1), axis=-1, keepdims=True)
    p1 = 1.0 / z
    p2 = jnp.exp(e2 - e1) / z
    w1 = g_w * p1 / (p1 + p2)
    w2 = g_w * p2 / (p1 + p2)
    id_ref[...] = jnp.where(lane == 0, i1 - N_GROUPS, jnp.where(lane == 1, i2 - N_GROUPS, 0))
    wt_ref[...] = jnp.where(lane == 0, w1, jnp.where(lane == 1, w2, 0.0))


def out_proj_and_route(mix, w_out, x, g_ffn, w_router, b_router, normed, row_offset=0):
    t, d = x.shape
    tm = min(256, t)
    assert row_offset % tm == 0
    row = lambda n: pl.BlockSpec((tm, n), lambda i: (i, 0))
    full = lambda a: pl.BlockSpec(a.shape, lambda i: (0, 0))
    g2, b2 = g_ffn.reshape(1, d), b_router.reshape(1, LANES)
    args = [mix, w_out, x, g2, w_router, b2, normed]
    in_specs = [row(d), full(w_out), row(d), full(g2), full(w_router), full(b2), pl.BlockSpec(memory_space=pl.ANY)]
    return pl.pallas_call(
        _route_kernel,
        grid=(t // tm,),
        in_specs=in_specs,
        out_specs=[row(d), pl.BlockSpec((tm, d), lambda i: (row_offset // tm + i, 0)), row(LANES), row(LANES)],
        out_shape=[jax.ShapeDtypeStruct((t, d), F32), jax.ShapeDtypeStruct(normed.shape, F32),
                   jax.ShapeDtypeStruct((t, LANES), I32), jax.ShapeDtypeStruct((t, LANES), F32)],
        input_output_aliases={6: 1},
        compiler_params=_params("parallel"),
        name="out_proj_and_route",
    )(*args)


def _rank_kernel(id_ref, rank_ref, cnt_ref):
    @pl.when(pl.program_id(0) == 0)
    def _():
        cnt_ref[...] = jnp.zeros_like(cnt_ref)

    ids = id_ref[...]
    n = ids.shape[0]
    lane = lax.broadcasted_iota(I32, ids.shape, 1)
    hit0 = lane == ids[:, 0:1]
    hit1 = lane == ids[:, 1:2]
    hits = (jnp.where(hit0, 1.0, 0.0) + jnp.where(hit1, 1.0, 0.0)).astype(BF16)
    lower = (lax.broadcasted_iota(I32, (n, n), 0) >= lax.broadcasted_iota(I32, (n, n), 1)).astype(BF16)
    seen = jnp.dot(lower, hits, preferred_element_type=F32) + cnt_ref[...]
    r0 = jnp.sum(jnp.where(hit0, seen, 0.0), axis=-1, keepdims=True) - 1.0
    r1 = jnp.sum(jnp.where(hit1, seen, 0.0), axis=-1, keepdims=True) - 1.0
    rank_ref[...] = jnp.where(lane == 0, r0, jnp.where(lane == 1, r1, 0.0)).astype(I32)
    cnt_ref[...] = seen[n - 1:n, :]


def rank_slots(ids):
    t = ids.shape[0]
    tm = 512
    return pl.pallas_call(
        _rank_kernel,
        grid=(t // tm,),
        in_specs=[pl.BlockSpec((tm, LANES), lambda i: (i, 0))],
        out_specs=[pl.BlockSpec((tm, LANES), lambda i: (i, 0)), pl.BlockSpec((1, LANES), lambda i: (0, 0))],
        out_shape=[jax.ShapeDtypeStruct((t, LANES), I32), jax.ShapeDtypeStruct((1, LANES), F32)],
        compiler_params=_params("arbitrary"),
        name="rank_slots",
    )(ids)


def _expert_kernel(be_ref, na_ref, first_ref, next_ref, grp_ref, tok_ref, x_ref, wg_hbm, wu_hbm, wd_hbm, o_ref,
                   buf_ref, sg_ref, su_ref, sd_ref, wg_ref, wu_ref, wd_ref, sem, wsem, *, layer):
    blk = pl.program_id(0)
    rows = buf_ref.shape[1]
    n_active = na_ref[0]

    @pl.when(blk == 0)
    def _():
        buf_ref[...] = jnp.zeros_like(buf_ref)

    def weight_copies(e):
        return (pltpu.make_async_copy(wg_hbm.at[layer, e], sg_ref, wsem.at[0]),
                pltpu.make_async_copy(wu_hbm.at[layer, e], su_ref, wsem.at[1]),
                pltpu.make_async_copy(wd_hbm.at[layer, e], sd_ref, wsem.at[2]))

    def row_copy(b, r):
        slot = b % 2
        tok = tok_ref[b * rows + r]
        return pltpu.make_async_copy(x_ref.at[pl.ds(tok, 1), :], buf_ref.at[slot, pl.ds(r, 1), :], sem.at[slot])

    def gather(b):
        def start(g, c):
            for j in range(GATHER_UNROLL):
                row_copy(b, g * GATHER_UNROLL + j).start()
            return c

        lax.fori_loop(0, grp_ref[b], start, 0)

    @pl.when(jnp.logical_and(blk == 0, n_active > 0))
    def _():
        for cp in weight_copies(be_ref[0]):
            cp.start(priority=1)
        gather(blk)

    @pl.when(blk + 1 < n_active)
    def _():
        gather(blk + 1)

    @pl.when(jnp.logical_and(blk < n_active, first_ref[blk] == 1))
    def _():
        for cp in weight_copies(be_ref[blk]):
            cp.wait()
        wg_ref[...] = sg_ref[...].astype(BF16)
        wu_ref[...] = su_ref[...].astype(BF16)
        wd_ref[...] = sd_ref[...].astype(BF16)

        @pl.when(next_ref[blk] >= 0)
        def _():
            for cp in weight_copies(next_ref[blk]):
                cp.start(priority=1)

    @pl.when(blk < n_active)
    def _():
        def wait(g, c):
            for j in range(GATHER_UNROLL):
                row_copy(blk, g * GATHER_UNROLL + j).wait()
            return c

        lax.fori_loop(0, grp_ref[blk], wait, 0)
        x = buf_ref[blk % 2].astype(BF16)
        gate = jnp.dot(x, wg_ref[...], preferred_element_type=F32)
        up = jnp.dot(x, wu_ref[...], preferred_element_type=F32)
        act = (gate * jax.nn.sigmoid(gate) * up).astype(BF16)
        o_ref[...] = jnp.dot(act, wd_ref[...], preferred_element_type=F32)

    @pl.when(blk >= na_ref[0])
    def _():
        o_ref[...] = jnp.zeros_like(o_ref)


def run_experts(block_expert, n_active, block_rows, row_token, x, w_gate, w_up, w_down, layer):
    n_blocks = block_expert.shape[0]
    _, _, d, ff = w_gate.shape
    rows = MOE_ROWS
    idx = jnp.arange(n_blocks, dtype=I32)
    first = jnp.concatenate([jnp.ones((1,), I32), (block_expert[1:] != block_expert[:-1]).astype(I32)])
    later = (idx[None, :] > idx[:, None]) & (block_expert[None, :] != block_expert[:, None]) & (idx[None, :] < n_active[0])
    nxt_blk = jnp.min(jnp.where(later, idx[None, :], n_blocks), axis=1)
    nxt = jnp.where(nxt_blk < n_blocks, block_expert[jnp.minimum(nxt_blk, n_blocks - 1)], -1).astype(I32)
    groups = (-(-block_rows // GATHER_UNROLL)).astype(I32)
    hbm = pl.BlockSpec(memory_space=pl.ANY)
    return pl.pallas_call(
        functools.partial(_expert_kernel, layer=layer),
        grid_spec=pltpu.PrefetchScalarGridSpec(
            num_scalar_prefetch=6,
            grid=(n_blocks,),
            in_specs=[hbm, hbm, hbm, hbm],
            out_specs=pl.BlockSpec((rows, d), lambda i, *_: (i, 0)),
            scratch_shapes=[pltpu.VMEM((2, rows, d), F32),
                            pltpu.VMEM((d, ff), F32), pltpu.VMEM((d, ff), F32), pltpu.VMEM((ff, d), F32),
                            pltpu.VMEM((d, ff), BF16), pltpu.VMEM((d, ff), BF16), pltpu.VMEM((ff, d), BF16),
                            pltpu.SemaphoreType.DMA((2,)), pltpu.SemaphoreType.DMA((3,))],
        ),
        out_shape=jax.ShapeDtypeStruct((n_blocks * rows, d), F32),
        compiler_params=_params("arbitrary"),
        name="run_experts",
    )(block_expert, n_active, first, nxt, groups, row_token, x, w_gate, w_up, w_down)


def _combine_kernel(dest_ref, h_ref, wt_ref, p_ref, gp_ref, wpg_ref, wpp_ref, gf_ref, y_ref, o_ref, buf_ref, sem):
    i = pl.program_id(0)
    tm = h_ref.shape[0]

    def row_copy(t, r, c):
        slot = t % 2
        src = dest_ref[(t * tm + r) * TOP_K + c]
        return pltpu.make_async_copy(y_ref.at[pl.ds(src, 1), :], buf_ref.at[slot, c, pl.ds(r, 1), :], sem.at[slot, c])

    def gather(t):
        def start(r, carry):
            for c in range(TOP_K):
                row_copy(t, r, c).start()
            return carry

        lax.fori_loop(0, tm, start, 0, unroll=GATHER_UNROLL)

    @pl.when(i == 0)
    def _():
        gather(i)

    @pl.when(i + 1 < pl.num_programs(0))
    def _():
        gather(i + 1)

    def wait(r, carry):
        for c in range(TOP_K):
            row_copy(i, r, c).wait()
        return carry

    lax.fori_loop(0, tm, wait, 0, unroll=GATHER_UNROLL)
    wt = wt_ref[...]
    slot = i % 2
    h2 = h_ref[...] + (wt[:, 0:1] * buf_ref[slot, 0] + wt[:, 1:2] * buf_ref[slot, 1])
    gate = jax.nn.sigmoid(jnp.dot(_rms(h2, gp_ref[...]).astype(BF16), wpg_ref[...], preferred_element_type=F32))
    ple = jnp.dot(p_ref[...].astype(BF16), wpp_ref[...], preferred_element_type=F32)
    o_ref[...] = _rms(h2 + gate * ple, gf_ref[...])


def combine_and_finish(dest, h1, wts, p, g_ple, w_ple_gate, w_ple_proj, g_final, y_rows):
    t, d = h1.shape
    tm = min(256, t)
    row = lambda n: pl.BlockSpec((tm, n), lambda i, ds: (i, 0))
    full = lambda a: pl.BlockSpec(a.shape, lambda i, ds: (0, 0))
    gp, gf = g_ple.reshape(1, d), g_final.reshape(1, d)
    return pl.pallas_call(
        _combine_kernel,
        grid_spec=pltpu.PrefetchScalarGridSpec(
            num_scalar_prefetch=1,
            grid=(t // tm,),
            in_specs=[row(d), row(LANES), row(p.shape[1]), full(gp), full(w_ple_gate), full(w_ple_proj), full(gf),
                      pl.BlockSpec(memory_space=pl.ANY)],
            out_specs=row(d),
            scratch_shapes=[pltpu.VMEM((2, TOP_K, tm, d), F32), pltpu.SemaphoreType.DMA((2, TOP_K))],
        ),
        out_shape=jax.ShapeDtypeStruct((t, d), F32),
        compiler_params=_params("arbitrary"),
        name="combine_and_finish",
    )(dest, h1, wts, p, gp, w_ple_gate, w_ple_proj, gf, y_rows)


def _segments(d):
    qk = d // 2
    sizes = dict(gq=qk, gk=qk, gv=d, gr=d, ga=GLA_RANK, fq=d, fk=d, fv=d, ff=d // FOX_DH, mg=d, mf=d)
    start, acc = {}, 0
    for name, size in sizes.items():
        start[name] = acc
        acc += size
    return sizes, start


def kernel(x_prompt, x_sample, cache_fox_k, cache_fox_v, cache_fox_logf, state_gla, page_table,
           p_prompt, p_sample, g_mix, w_in, w_gla_a2, b_gla_a, b_fox_f, g_gla_out, w_o_gla, w_o_fox, w_out,
           g_ffn, w_route_group, b_route_group, w_route_expert, b_route_expert, w_exp_gate, w_exp_up,
           w_exp_down, g_ple, w_ple_gate, w_ple_proj, g_final):
    depth = w_in.shape[0]
    bp, sp, d = x_prompt.shape
    bs, ss, _ = x_sample.shape
    assert ss == 1, "the sample group carries one new token per sequence"
    fox_heads = d // FOX_DH
    n_phys, page_size = cache_fox_k.shape[1:3]
    tp, ts = bp * sp, bs * ss
    sizes, start = _segments(d)
    main = ("gq", "gk", "gv", "gr", "fq", "mg", "mf", "fk", "fv")
    off, acc = {}, 0
    for name in main:
        off[name] = acc
        acc += sizes[name]

    hp = x_prompt.reshape(tp, d)
    hs = x_sample.reshape(ts, d)
    outs = {name: [] for name in ("kp", "vp", "lfp", "sp", "ks", "vs", "lfs", "ss")}
    for i in range(depth):
        w = w_in[i]
        seg = lambda name: w[:, start[name]:start[name] + sizes[name]]
        w_main = jnp.concatenate([seg(n) for n in main], axis=1).astype(BF16)
        w_small = jnp.pad(jnp.concatenate([seg("ga"), seg("ff")], axis=1),
                          ((0, 0), (0, LANES - GLA_RANK - fox_heads))).astype(BF16)
        w_a2_pad = jnp.pad(w_gla_a2[i], ((0, LANES - GLA_RANK), (0, 0))).astype(BF16)
        w_router = jnp.pad(jnp.concatenate([w_route_group[i], w_route_expert[i]], axis=1),
                           ((0, 0), (0, LANES - N_GROUPS - N_EXPERTS))).astype(BF16)
        b_router = jnp.pad(jnp.concatenate([b_route_group[i], b_route_expert[i]]),
                           (0, LANES - N_GROUPS - N_EXPERTS))
        wog, wof, wo = w_o_gla[i].astype(BF16), w_o_fox[i].astype(BF16), w_out[i].astype(BF16)

        def forget(z_small, batch, seq):
            f_t = z_small[:, GLA_RANK:GLA_RANK + fox_heads].reshape(batch, seq, fox_heads).transpose(0, 2, 1)
            return fox_log_forget(f_t, b_fox_f[i])

        us = rms_cast(hs, g_mix[i])
        zs, zss = matmul(us, w_main), matmul(us, w_small, tn=LANES)
        las = gla_log_decay(zss, w_a2_pad, b_gla_a[i])
        fks, fvs = zs[:, off["fk"]:off["fk"] + d], zs[:, off["fv"]:off["fv"] + d]
        lfs_t, _ = forget(zss, 1, ts)
        lfs = lfs_t[0].T
        ogs, s_new = gla_sample(zs, las, g_gla_out[i], state_gla[i], off)
        bias = fox_sample_bias(page_table, lfs.reshape(bs, fox_heads, 1),
                               cache_fox_logf[i].transpose(0, 2, 1))
        n_pages = page_table.shape[1]
        bias = bias.transpose(0, 1, 3, 2).reshape(bs, n_pages // FOX_PAGES_PER_STEP, FOX_PAGES_PER_STEP,
                                                  page_size * fox_heads)
        heads = lambda a: a.reshape(bs, fox_heads, FOX_DH)
        decode = (page_table, heads(zs[:, off["fq"]:off["fq"] + d]), heads(fks), heads(fvs), bias,
                  cache_fox_k, cache_fox_v, i)

        up = rms_cast(hp, g_mix[i])
        zsp = matmul(up, w_small, tn=LANES)
        if projection_fits_decode(page_table, up, w_main):
            ofs, zp = fox_sample(*decode, proj=(up, w_main))
        else:
            ofs, zp = fox_sample(*decode), matmul(up, w_main)
        ofs = ofs.reshape(bs, d)
        lap = gla_log_decay(zsp, w_a2_pad, b_gla_a[i])
        lf_t, c_t = forget(zsp, bp, sp)
        ogp, s_fin = gla_prompt(zp, lap, g_gla_out[i], bp, sp, off)
        ofp, fkp, fvp = fox_prompt(zp, c_t.reshape(bp * fox_heads, 1, sp), bp, sp, off, fox_heads)
        mixp = merge_branches(ogp, ofp, wog, wof, zp, off["mg"], off["mf"])
        h1p, hn_all, idp, wtp = out_proj_and_route(mixp, wo, hp, g_ffn[i], w_router, b_router,
                                                   jnp.zeros((tp + ts, d), F32))
        outs["kp"].append(fkp.reshape(bp, sp, fox_heads, FOX_DH))
        outs["vp"].append(fvp.reshape(bp, sp, fox_heads, FOX_DH))
        outs["lfp"].append(lf_t.transpose(0, 2, 1))
        outs["sp"].append(s_fin)

        mixs = merge_branches(ogs, ofs, wog, wof, zs, off["mg"], off["mf"])
        h1s, hn_all, ids, wts = out_proj_and_route(mixs, wo, hs, g_ffn[i], w_router, b_router, hn_all,
                                                   row_offset=tp)
        outs["ks"].append(fks.reshape(bs, ss, fox_heads, FOX_DH))
        outs["vs"].append(fvs.reshape(bs, ss, fox_heads, FOX_DH))
        outs["lfs"].append(lfs.reshape(bs, ss, fox_heads))
        outs["ss"].append(s_new)

        t_all = tp + ts
        n_slots = t_all * TOP_K
        t_pad = -(-t_all // 512) * 512
        ids_all = jnp.concatenate([idp, ids], axis=0)
        ids_pad = jnp.pad(ids_all, ((0, t_pad - t_all), (0, 0)), constant_values=LANES - 1)
        rank, counts = rank_slots(ids_pad)
        counts = counts[0, :N_EXPERTS].astype(I32)
        padded = -(-counts // MOE_ROWS) * MOE_ROWS
        ends = jnp.cumsum(padded)
        experts = ids_all[:, :TOP_K]
        dest = (ends - padded)[experts] + rank[:t_all, :TOP_K]
        n_blocks = -(-n_slots // MOE_ROWS) + N_EXPERTS
        n_active = (ends[-1] // MOE_ROWS).reshape(1)
        block_start = jnp.arange(n_blocks, dtype=I32) * MOE_ROWS
        block_expert = jnp.searchsorted(ends, jnp.minimum(block_start, ends[-1] - 1), side="right").astype(I32)
        tokens = jnp.broadcast_to(jnp.arange(t_all, dtype=I32)[:, None], (t_all, TOP_K))
        row_token = jnp.zeros((n_blocks * MOE_ROWS,), I32).at[dest.reshape(-1)].set(tokens.reshape(-1))
        block_rows = jnp.clip(counts[block_expert] - (block_start - (ends - padded)[block_expert]), 0, MOE_ROWS)
        y_rows = run_experts(block_expert, n_active.astype(I32), block_rows, row_token, hn_all,
                             w_exp_gate, w_exp_up, w_exp_down, i)

        last = i == depth - 1
        assert last, "the final norm is fused into the last layer's finishing kernel"
        fin = (g_ple[i], w_ple_gate[i].astype(BF16), w_ple_proj[i].astype(BF16), g_final, y_rows)
        hp = combine_and_finish(dest[:tp].reshape(-1), h1p, wtp, p_prompt[i].reshape(tp, -1), *fin)
        hs = combine_and_finish(dest[tp:].reshape(-1), h1s, wts, p_sample[i].reshape(ts, -1), *fin)

    stack = lambda name: jnp.stack(outs[name])
    return (hp.reshape(bp, sp, d), hs.reshape(bs, ss, d),
            stack("kp"), stack("vp"), stack("lfp"), stack("sp"),
            stack("ks"), stack("vs"), stack("lfs"), stack("ss"))
```

```python
import functools

import jax
import jax.numpy as jnp
from jax import lax
from jax.experimental import pallas as pl
from jax.experimental.pallas import tpu as pltpu

F32 = jnp.float32
BF16 = jnp.bfloat16
I32 = jnp.int32
HIGHEST = lax.Precision.HIGHEST
NEG_INF = float("-inf")

RMS_EPS = 1e-6
GLA_HEADS = 4
GLA_RANK = 16
GLA_TAU = 16.0
GLA_CHUNK = 64
GLA_SUB = 16
FOX_DH = 128
N_GROUPS = 4
EXPERTS_PER_GROUP = 8
N_EXPERTS = N_GROUPS * EXPERTS_PER_GROUP
TOP_K = 2

LANES = 128
MOE_ROWS = 256
GATHER_UNROLL = 8
VMEM_LIMIT = 56 * 2**20


def _params(*sem):
    return pltpu.CompilerParams(dimension_semantics=sem, vmem_limit_bytes=VMEM_LIMIT)


def _log_sigmoid(x):
    return jnp.minimum(x, 0.0) - jnp.log1p(jnp.exp(-jnp.abs(x)))


def _rms(x, g):
    return x * lax.rsqrt(jnp.mean(x * x, axis=-1, keepdims=True) + RMS_EPS) * g


def _nt_dot(a, b):
    return lax.dot_general(a, b, (((1,), (1,)), ((), ())), preferred_element_type=F32)


def _tn_dot(a, b):
    return lax.dot_general(a, b, (((0,), (0,)), ((), ())), preferred_element_type=F32)


def _row_to_col(row):
    n = row.shape[1]
    eye = lax.broadcasted_iota(I32, (n, n), 0) == lax.broadcasted_iota(I32, (n, n), 1)
    return jnp.sum(jnp.where(eye, row, 0.0), axis=1, keepdims=True)


def _rms_cast_kernel(x_ref, g_ref, o_ref):
    o_ref[...] = _rms(x_ref[...], g_ref[...]).astype(o_ref.dtype)


def rms_cast(x, g):
    t, d = x.shape
    tm = min(512, t)
    return pl.pallas_call(
        _rms_cast_kernel,
        grid=(t // tm,),
        in_specs=[pl.BlockSpec((tm, d), lambda i: (i, 0)), pl.BlockSpec((1, d), lambda i: (0, 0))],
        out_specs=pl.BlockSpec((tm, d), lambda i: (i, 0)),
        out_shape=jax.ShapeDtypeStruct((t, d), BF16),
        compiler_params=_params("parallel"),
        name="rms_cast",
    )(x, g.reshape(1, d))


def _mm_kernel(a_ref, w_ref, o_ref):
    o_ref[...] = jnp.dot(a_ref[...], w_ref[...], preferred_element_type=F32).astype(o_ref.dtype)


def matmul(a, w, out_dtype=F32, tm=512, tn=1024):
    t, k = a.shape
    n = w.shape[1]
    tm, tn = min(tm, t), min(tn, n)
    return pl.pallas_call(
        _mm_kernel,
        grid=(n // tn, t // tm),
        in_specs=[pl.BlockSpec((tm, k), lambda j, i: (i, 0)), pl.BlockSpec((k, tn), lambda j, i: (0, j))],
        out_specs=pl.BlockSpec((tm, tn), lambda j, i: (i, j)),
        out_shape=jax.ShapeDtypeStruct((t, n), out_dtype),
        compiler_params=_params("parallel", "parallel"),
        name="matmul",
    )(a, w)


def _la_kernel(z_ref, w_ref, b_ref, o_ref):
    pre = jnp.dot(z_ref[...].astype(BF16), w_ref[...], preferred_element_type=F32) + b_ref[...]
    o_ref[...] = _log_sigmoid(pre) / GLA_TAU


def gla_log_decay(z_small, w_a2_pad, b_a):
    t = z_small.shape[0]
    n = w_a2_pad.shape[1]
    tm = min(512, t)
    return pl.pallas_call(
        _la_kernel,
        grid=(t // tm,),
        in_specs=[pl.BlockSpec((tm, LANES), lambda i: (i, 0)),
                  pl.BlockSpec((LANES, n), lambda i: (0, 0)),
                  pl.BlockSpec((1, n), lambda i: (0, 0))],
        out_specs=pl.BlockSpec((tm, n), lambda i: (i, 0)),
        out_shape=jax.ShapeDtypeStruct((t, n), F32),
        compiler_params=_params("parallel"),
        name="gla_log_decay",
    )(z_small, w_a2_pad, b_a.reshape(1, n))


def _logf_kernel(f_ref, b_ref, lf_ref, c_ref, *, blk):
    lf = _log_sigmoid(f_ref[...] + b_ref[...])
    lf_ref[...] = lf
    s = lf.shape[1]
    upper = (lax.broadcasted_iota(I32, (blk, blk), 0) <= lax.broadcasted_iota(I32, (blk, blk), 1)).astype(F32)
    carry = jnp.zeros((lf.shape[0], 1), F32)
    for j in range(s // blk):
        cs = jnp.dot(lf[:, j * blk:(j + 1) * blk], upper, precision=HIGHEST, preferred_element_type=F32) + carry
        c_ref[:, j * blk:(j + 1) * blk] = cs
        carry = cs[:, blk - 1:blk]


def fox_log_forget(f_t, b_f):
    b, h, s = f_t.shape
    blk = min(256, s)
    spec = pl.BlockSpec((None, h, s), lambda i: (i, 0, 0))
    return pl.pallas_call(
        functools.partial(_logf_kernel, blk=blk),
        grid=(b,),
        in_specs=[spec, pl.BlockSpec((h, 1), lambda i: (0, 0))],
        out_specs=[spec, spec],
        out_shape=[jax.ShapeDtypeStruct((b, h, s), F32)] * 2,
        compiler_params=_params("parallel"),
        name="fox_log_forget",
    )(f_t, b_f.reshape(h, 1))


def _gla_prompt_kernel(q_ref, k_ref, v_ref, r_ref, zs_ref, wa_ref, ba_ref, g_ref, o_ref, s_ref, *, rows, heads):
    @pl.when(pl.program_id(2) == 0)
    def _():
        s_ref[...] = jnp.zeros_like(s_ref)

    c, sub = GLA_CHUNK, GLA_SUB
    nsub = c // sub
    dk = q_ref.shape[1] // heads
    dv = v_ref.shape[1] // heads
    scale = dk ** -0.5
    row = lax.broadcasted_iota(I32, (c, c), 0)
    col = lax.broadcasted_iota(I32, (c, c), 1)
    lower = (row >= col).astype(F32)
    col_sub = lax.broadcasted_iota(I32, (sub, c), 1)
    row_in_sub = lax.broadcasted_iota(I32, (c, 1), 0) % sub

    def sub_rows(x, s):
        return jnp.concatenate(
            [jnp.broadcast_to(x[i * sub + s:i * sub + s + 1, :], (sub, dk)) for i in range(nsub)], axis=0)

    def one_head(sl, hh, la):
        kcols = slice(hh * dk, (hh + 1) * dk)
        vcols = slice(hh * dv, (hh + 1) * dv)
        q = q_ref[sl, kcols] * scale
        k = k_ref[sl, kcols]
        vb = v_ref[sl, vcols].astype(BF16)
        b = jnp.dot(lower, la[:, kcols], precision=HIGHEST, preferred_element_type=F32)
        state = s_ref[hh]

        o = jnp.dot((q * jnp.exp(b)).astype(BF16), state.astype(BF16), preferred_element_type=F32)

        a = jnp.zeros((c, c), F32)
        for s in range(sub):
            e = jnp.exp(jnp.where(row_in_sub >= s, b - sub_rows(b, s), NEG_INF))
            w = jnp.sum(q * e * sub_rows(k, s), axis=-1, keepdims=True)
            a = jnp.where(col == (row // sub) * sub + s, w, a)
        blocks = [a[:sub]]
        for i in range(1, nsub):
            ri = slice(i * sub, (i + 1) * sub)
            edge = b[i * sub - 1:i * sub, :]
            qs = (q[ri] * jnp.exp(b[ri] - edge)).astype(BF16)
            ks = (k * jnp.exp(jnp.minimum(edge - b, 0.0))).astype(BF16)
            blocks.append(jnp.where(col_sub < i * sub, _nt_dot(qs, ks), a[ri]))
        a = jnp.concatenate(blocks, axis=0)
        o = o + jnp.dot(a.astype(BF16), vb, preferred_element_type=F32)

        r = r_ref[sl, vcols]
        o_ref[sl, vcols] = (_rms(o, g_ref[...]) * (r * jax.nn.sigmoid(r))).astype(o_ref.dtype)

        last = b[c - 1:c, :]
        kd = (k * jnp.exp(last - b)).astype(BF16)
        s_ref[hh] = _row_to_col(jnp.exp(last)) * state + _tn_dot(kd, vb)

    def chunk(ci, carry):
        sl = pl.ds(pl.multiple_of(ci * c, c), c)
        pre = jnp.dot(zs_ref[sl, :].astype(BF16), wa_ref[...], preferred_element_type=F32) + ba_ref[...]
        la = _log_sigmoid(pre) / GLA_TAU
        for hh in range(heads):
            one_head(sl, hh, la)
        return carry

    lax.fori_loop(0, rows // c, chunk, 0)


GLA_HEADS_PER_STEP = 4


def gla_prompt(z, z_small, w_a2_pad, b_a, g_out, batch, seq, off):
    t = z.shape[0]
    hp = GLA_HEADS_PER_STEP
    dk = w_a2_pad.shape[1] // GLA_HEADS
    dv = g_out.shape[0]
    wk, wv = hp * dk, hp * dv
    rows = min(512, seq)
    nt = seq // rows
    kern = functools.partial(_gla_prompt_kernel, rows=rows, heads=hp)
    rmap = lambda b, h, i: b * nt + i
    return pl.pallas_call(
        kern,
        grid=(batch, GLA_HEADS // hp, nt),
        in_specs=[
            pl.BlockSpec((rows, wk), lambda b, h, i: (rmap(b, h, i), off["gq"] // wk + h)),
            pl.BlockSpec((rows, wk), lambda b, h, i: (rmap(b, h, i), off["gk"] // wk + h)),
            pl.BlockSpec((rows, wv), lambda b, h, i: (rmap(b, h, i), off["gv"] // wv + h)),
            pl.BlockSpec((rows, wv), lambda b, h, i: (rmap(b, h, i), off["gr"] // wv + h)),
            pl.BlockSpec((rows, LANES), lambda b, h, i: (rmap(b, h, i), 0)),
            pl.BlockSpec((LANES, wk), lambda b, h, i: (0, h)),
            pl.BlockSpec((1, wk), lambda b, h, i: (0, h)),
            pl.BlockSpec((1, dv), lambda b, h, i: (0, 0)),
        ],
        out_specs=[
            pl.BlockSpec((rows, wv), lambda b, h, i: (rmap(b, h, i), h)),
            pl.BlockSpec((None, hp, dk, dv), lambda b, h, i: (b, h, 0, 0)),
        ],
        out_shape=[jax.ShapeDtypeStruct((t, GLA_HEADS * dv), BF16),
                   jax.ShapeDtypeStruct((batch, GLA_HEADS, dk, dv), F32)],
        compiler_params=_params("parallel", "parallel", "arbitrary"),
        name="gla_prompt",
    )(z, z, z, z, z_small, w_a2_pad, b_a.reshape(1, -1), g_out.reshape(1, dv))


def _gla_sample_kernel(q_ref, k_ref, v_ref, r_ref, la_ref, g_ref, s_ref, o_ref, sn_ref):
    h, dk, dv = s_ref.shape
    for j in range(h):
        kc = slice(j * dk, (j + 1) * dk)
        vc = slice(j * dv, (j + 1) * dv)
        q = q_ref[:, kc] * dk ** -0.5
        k = k_ref[:, kc]
        v = v_ref[:, vc]
        decay = jnp.exp(la_ref[:, kc])
        state = s_ref[j]
        sn_ref[j] = _row_to_col(decay) * state + _row_to_col(k) * v
        qe = _row_to_col((q * decay).astype(BF16).astype(F32))
        o = jnp.sum(qe * state.astype(BF16).astype(F32), axis=0, keepdims=True)
        o = o + jnp.sum(q * k, axis=1, keepdims=True) * v
        r = r_ref[:, vc]
        o_ref[:, vc] = (_rms(o, g_ref[...]) * (r * jax.nn.sigmoid(r))).astype(o_ref.dtype)


def gla_sample(z, la, g_out, state, off):
    b = z.shape[0]
    _, h, dk, dv = state.shape
    wk, wv = h * dk, h * dv
    z3 = z.reshape(b, 1, z.shape[1])
    la3 = la.reshape(b, 1, la.shape[1])
    og, sn = pl.pallas_call(
        _gla_sample_kernel,
        grid=(b,),
        in_specs=[
            pl.BlockSpec((None, 1, wk), lambda i: (i, 0, off["gq"] // wk)),
            pl.BlockSpec((None, 1, wk), lambda i: (i, 0, off["gk"] // wk)),
            pl.BlockSpec((None, 1, wv), lambda i: (i, 0, off["gv"] // wv)),
            pl.BlockSpec((None, 1, wv), lambda i: (i, 0, off["gr"] // wv)),
            pl.BlockSpec((None, 1, wk), lambda i: (i, 0, 0)),
            pl.BlockSpec((1, dv), lambda i: (0, 0)),
            pl.BlockSpec((None, h, dk, dv), lambda i: (i, 0, 0, 0)),
        ],
        out_specs=[
            pl.BlockSpec((None, 1, wv), lambda i: (i, 0, 0)),
            pl.BlockSpec((None, h, dk, dv), lambda i: (i, 0, 0, 0)),
        ],
        out_shape=[jax.ShapeDtypeStruct((b, 1, wv), BF16), jax.ShapeDtypeStruct(state.shape, F32)],
        compiler_params=_params("parallel"),
        name="gla_sample",
    )(z3, z3, z3, z3, la3, g_out.reshape(1, dv), state)
    return og.reshape(b, wv), sn


def _fox_prompt_kernel(q_ref, k_ref, v_ref, c_ref, o_ref, ko_ref, vo_ref, kb_ref, vb_ref, *, blk):
    s, dh = q_ref.shape
    ko_ref[...] = k_ref[...]
    vo_ref[...] = v_ref[...]
    kb_ref[...] = k_ref[...].astype(BF16)
    vb_ref[...] = v_ref[...].astype(BF16)
    causal = lax.broadcasted_iota(I32, (blk, blk), 0) >= lax.broadcasted_iota(I32, (blk, blk), 1)
    for i in range(s // blk):
        qs = slice(i * blk, (i + 1) * blk)
        q = (q_ref[qs, :] * dh ** -0.5).astype(BF16)
        m = jnp.full((blk, 1), NEG_INF, F32)
        l = jnp.zeros((blk, 1), F32)
        acc = jnp.zeros((blk, dh), F32)
        for j in range(i + 1):
            ks = slice(j * blk, (j + 1) * blk)
            logit = _nt_dot(q, kb_ref[ks, :]) - c_ref[:, ks]
            if j == i:
                logit = jnp.where(causal, logit, NEG_INF)
            m_new = jnp.maximum(m, jnp.max(logit, axis=-1, keepdims=True))
            alpha = jnp.exp(m - m_new)
            p = jnp.exp(logit - m_new)
            l = alpha * l + jnp.sum(p, axis=-1, keepdims=True)
            acc = alpha * acc + jnp.dot(p.astype(BF16), vb_ref[ks, :], preferred_element_type=F32)
            m = m_new
        o_ref[qs, :] = (acc / l).astype(o_ref.dtype)


def fox_prompt(z, c_t, batch, seq, off, h):
    t = z.shape[0]
    w = h * FOX_DH
    blk = min(512, seq)
    head = pl.BlockSpec((seq, FOX_DH), lambda b, j: (b, j))
    return pl.pallas_call(
        functools.partial(_fox_prompt_kernel, blk=blk),
        grid=(batch, h),
        in_specs=[
            pl.BlockSpec((seq, FOX_DH), lambda b, j: (b, off["fq"] // FOX_DH + j)),
            pl.BlockSpec((seq, FOX_DH), lambda b, j: (b, off["fk"] // FOX_DH + j)),
            pl.BlockSpec((seq, FOX_DH), lambda b, j: (b, off["fv"] // FOX_DH + j)),
            pl.BlockSpec((None, 1, seq), lambda b, j: (b * h + j, 0, 0)),
        ],
        out_specs=[head, head, head],
        out_shape=[jax.ShapeDtypeStruct((t, w), BF16), jax.ShapeDtypeStruct((t, w), F32),
                   jax.ShapeDtypeStruct((t, w), F32)],
        scratch_shapes=[pltpu.VMEM((seq, FOX_DH), BF16)] * 2,
        compiler_params=_params("parallel", "parallel"),
        name="fox_prompt",
    )(z, z, z, c_t)


def _fox_bias_kernel(pt_ref, lfn_ref, pool_ref, o_ref, buf_ref, sem):
    b = pl.program_id(0)
    npg, h, ps = buf_ref.shape

    def page_copy(p):
        return pltpu.make_async_copy(pool_ref.at[pt_ref[b, p]], buf_ref.at[p], sem)

    for p in range(npg):
        page_copy(p).start()
    for p in range(npg):
        page_copy(p).wait()

    x = buf_ref[...]
    after = (lax.broadcasted_iota(I32, (ps, ps), 0) >= lax.broadcasted_iota(I32, (ps, ps), 1)).astype(F32)
    suffix = jnp.dot(x.reshape(npg * h, ps), after, precision=HIGHEST,
                     preferred_element_type=F32).reshape(npg, h, ps)
    run = lfn_ref[...]
    for p in range(npg - 1, -1, -1):
        o_ref[p] = suffix[p] - x[p] + run
        run = run + suffix[p][:, 0:1]


def fox_sample_bias(page_table, lf_new, pool_lf_t):
    b, npg = page_table.shape
    _, h, ps = pool_lf_t.shape
    return pl.pallas_call(
        _fox_bias_kernel,
        grid_spec=pltpu.PrefetchScalarGridSpec(
            num_scalar_prefetch=1,
            grid=(b,),
            in_specs=[pl.BlockSpec((None, h, 1), lambda i, pt: (i, 0, 0)),
                      pl.BlockSpec(memory_space=pl.ANY)],
            out_specs=pl.BlockSpec((None, npg, h, ps), lambda i, pt: (i, 0, 0, 0)),
            scratch_shapes=[pltpu.VMEM((npg, h, ps), F32), pltpu.SemaphoreType.DMA(())],
        ),
        out_shape=jax.ShapeDtypeStruct((b, npg, h, ps), F32),
        compiler_params=_params("arbitrary"),
        name="fox_sample_bias",
    )(page_table, lf_new, pool_lf_t)


def _fox_sample_kernel(pt_ref, q_ref, kn_ref, vn_ref, bias_ref, *refs, pages, with_matmul):
    k_refs, v_refs = refs[:pages], refs[pages:2 * pages]
    if with_matmul:
        a_ref, w_ref, o_ref, z_ref, m_ref, l_ref, acc_ref = refs[2 * pages:]
        z_ref[...] = jnp.dot(a_ref[...], w_ref[...], preferred_element_type=F32)
    else:
        o_ref, m_ref, l_ref, acc_ref = refs[2 * pages:]
    p = pl.program_id(1)
    h, dh = q_ref.shape
    n = k_refs[0].shape[0] * h
    q = q_ref[...] * dh ** -0.5

    @pl.when(p == 0)
    def _():
        m_ref[...] = jnp.broadcast_to(jnp.sum(q * kn_ref[...], axis=1, keepdims=True), m_ref.shape)
        l_ref[...] = jnp.ones_like(l_ref)
        acc_ref[...] = vn_ref[...]

    own = lax.broadcasted_iota(I32, (h, n), 1) % h == lax.broadcasted_iota(I32, (h, n), 0)
    qb = q.astype(BF16)
    logits = [jnp.where(own, _nt_dot(qb, k_refs[r][...].reshape(n, dh).astype(BF16)) + bias_ref[r:r + 1, :], NEG_INF)
              for r in range(pages)]
    m = m_ref[:, 0:1]
    m_new = m
    for lg in logits:
        m_new = jnp.maximum(m_new, jnp.max(lg, axis=-1, keepdims=True))
    alpha = jnp.exp(m - m_new)
    l = alpha * l_ref[:, 0:1]
    acc = alpha * acc_ref[...]
    for r in range(pages):
        pr = jnp.exp(logits[r] - m_new)
        l = l + jnp.sum(pr, axis=-1, keepdims=True)
        acc = acc + jnp.dot(pr.astype(BF16), v_refs[r][...].reshape(n, dh).astype(BF16), preferred_element_type=F32)
    m_ref[...] = jnp.broadcast_to(m_new, m_ref.shape)
    l_ref[...] = jnp.broadcast_to(l, l_ref.shape)
    acc_ref[...] = acc

    @pl.when(p == pl.num_programs(1) - 1)
    def _():
        o_ref[...] = (acc / l).astype(o_ref.dtype)


FOX_PAGES_PER_STEP = 8


PROJ_TILE = (512, 1024)


def projection_fits_decode(page_table, a, w):
    tm, tn = PROJ_TILE
    steps = page_table.shape[0] * (page_table.shape[1] // FOX_PAGES_PER_STEP)
    return a.shape[0] % tm == 0 and w.shape[1] % tn == 0 and (a.shape[0] // tm) * (w.shape[1] // tn) == steps


def fox_sample(page_table, q, k_new, v_new, bias, cache_k, cache_v, layer, proj=None):
    b, npg = page_table.shape
    _, _, ps, h, dh = cache_k.shape
    g = FOX_PAGES_PER_STEP
    row = pl.BlockSpec((None, h, dh), lambda i, p, pt: (i, 0, 0))
    extra_in, extra_args, out_specs = [], [], row
    out_shape = jax.ShapeDtypeStruct((b, h, dh), BF16)
    if proj is not None:
        a, w = proj
        tm, tn = PROJ_TILE
        k = a.shape[1]
        nr = a.shape[0] // tm
        tile = lambda i, p: i * (npg // g) + p
        extra_in = [pl.BlockSpec((tm, k), lambda i, p, pt: (tile(i, p) % nr, 0)),
                    pl.BlockSpec((k, tn), lambda i, p, pt: (0, tile(i, p) // nr))]
        extra_args = [a, w]
        out_specs = [row, pl.BlockSpec((tm, tn), lambda i, p, pt: (tile(i, p) % nr, tile(i, p) // nr))]
        out_shape = [out_shape, jax.ShapeDtypeStruct((a.shape[0], w.shape[1]), F32)]

    def page(r):
        return pl.BlockSpec((None, None, ps, h, dh), lambda i, p, pt: (layer, pt[i, p * g + r], 0, 0, 0))

    pages = [page(r) for r in range(g)]
    return pl.pallas_call(
        functools.partial(_fox_sample_kernel, pages=g, with_matmul=proj is not None),
        grid_spec=pltpu.PrefetchScalarGridSpec(
            num_scalar_prefetch=1,
            grid=(b, npg // g),
            in_specs=[row, row, row,
                      pl.BlockSpec((None, None, g, ps * h), lambda i, p, pt: (i, p, 0, 0)),
                      *pages, *pages, *extra_in],
            out_specs=out_specs,
            scratch_shapes=[pltpu.VMEM((h, LANES), F32), pltpu.VMEM((h, LANES), F32), pltpu.VMEM((h, dh), F32)],
        ),
        out_shape=out_shape,
        compiler_params=_params("arbitrary", "arbitrary"),
        name="fox_sample",
    )(page_table, q, k_new, v_new, bias, *([cache_k] * g), *([cache_v] * g), *extra_args)


def _merge_kernel(og_ref, of_ref, wg_ref, wf_ref, mg_ref, mf_ref, o_ref):
    yg = jnp.dot(og_ref[...], wg_ref[...], preferred_element_type=F32)
    yf = jnp.dot(of_ref[...], wf_ref[...], preferred_element_type=F32)
    o_ref[...] = (jax.nn.sigmoid(mg_ref[...]) * yg + jax.nn.sigmoid(mf_ref[...]) * yf).astype(o_ref.dtype)


def merge_branches(og, of, w_o_gla, w_o_fox, z, mg_off, mf_off):
    t, k = og.shape
    n = w_o_gla.shape[1]
    tm, tn = min(512, t), min(1024, n)
    return pl.pallas_call(
        _merge_kernel,
        grid=(n // tn, t // tm),
        in_specs=[pl.BlockSpec((tm, k), lambda j, i: (i, 0)),
                  pl.BlockSpec((tm, k), lambda j, i: (i, 0)),
                  pl.BlockSpec((k, tn), lambda j, i: (0, j)),
                  pl.BlockSpec((k, tn), lambda j, i: (0, j)),
                  pl.BlockSpec((tm, tn), lambda j, i: (i, mg_off // tn + j)),
                  pl.BlockSpec((tm, tn), lambda j, i: (i, mf_off // tn + j))],
        out_specs=pl.BlockSpec((tm, tn), lambda j, i: (i, j)),
        out_shape=jax.ShapeDtypeStruct((t, n), BF16),
        compiler_params=_params("parallel", "parallel"),
        name="merge_branches",
    )(og, of, w_o_gla, w_o_fox, z, z)


def _route_kernel(mix_ref, w_ref, x_ref, g_ref, wr_ref, br_ref, *refs):
    h_ref, hn_ref, id_ref, wt_ref = refs[-4:]
    h1 = x_ref[...] + jnp.dot(mix_ref[...], w_ref[...], preferred_element_type=F32)
    h_ref[...] = h1
    hn = _rms(h1, g_ref[...])
    hn_ref[...] = hn
    logits = jnp.dot(hn.astype(BF16), wr_ref[...], preferred_element_type=F32) + br_ref[...]
    lane = lax.broadcasted_iota(I32, logits.shape, 1)
    big = jnp.int32(2 ** 30)

    def top(vals):
        best = jnp.max(vals, axis=-1, keepdims=True)
        return best, jnp.min(jnp.where(vals == best, lane, big), axis=-1, keepdims=True)

    grp = jnp.where(lane < N_GROUPS, logits, NEG_INF)
    g_max, g_idx = top(grp)
    g_w = 1.0 / jnp.sum(jnp.exp(grp - g_max), axis=-1, keepdims=True)
    lo = N_GROUPS + g_idx * EXPERTS_PER_GROUP
    mine = jnp.where(lane >= lo, jnp.where(lane < lo + EXPERTS_PER_GROUP, logits, NEG_INF), NEG_INF)
    e1, i1 = top(mine)
    e2, i2 = top(jnp.where(lane == i1, NEG_INF, mine))
    z = jnp.sum(jnp.exp(mine - e1), axis=-1, keepdims=True)
    p1 = 1.0 / z
    p2 = jnp.exp(e2 - e1) / z
    w1 = g_w * p1 / (p1 + p2)
    w2 = g_w * p2 / (p1 + p2)
    id_ref[...] = jnp.where(lane == 0, i1 - N_GROUPS, jnp.where(lane == 1, i2 - N_GROUPS, 0))
    wt_ref[...] = jnp.where(lane == 0, w1, jnp.where(lane == 1, w2, 0.0))


def out_proj_and_route(mix, w_out, x, g_ffn, w_router, b_router, normed, row_offset=0):
    t, d = x.shape
    tm = min(256, t)
    assert row_offset % tm == 0
    row = lambda n: pl.BlockSpec((tm, n), lambda i: (i, 0))
    full = lambda a: pl.BlockSpec(a.shape, lambda i: (0, 0))
    g2, b2 = g_ffn.reshape(1, d), b_router.reshape(1, LANES)
    args = [mix, w_out, x, g2, w_router, b2, normed]
    in_specs = [row(d), full(w_out), row(d), full(g2), full(w_router), full(b2), pl.BlockSpec(memory_space=pl.ANY)]
    return pl.pallas_call(
        _route_kernel,
        grid=(t // tm,),
        in_specs=in_specs,
        out_specs=[row(d), pl.BlockSpec((tm, d), lambda i: (row_offset // tm + i, 0)), row(LANES), row(LANES)],
        out_shape=[jax.ShapeDtypeStruct((t, d), F32), jax.ShapeDtypeStruct(normed.shape, F32),
                   jax.ShapeDtypeStruct((t, LANES), I32), jax.ShapeDtypeStruct((t, LANES), F32)],
        input_output_aliases={6: 1},
        compiler_params=_params("parallel"),
        name="out_proj_and_route",
    )(*args)


def _rank_kernel(id_ref, rank_ref, cnt_ref):
    @pl.when(pl.program_id(0) == 0)
    def _():
        cnt_ref[...] = jnp.zeros_like(cnt_ref)

    ids = id_ref[...]
    n = ids.shape[0]
    lane = lax.broadcasted_iota(I32, ids.shape, 1)
    hit0 = lane == ids[:, 0:1]
    hit1 = lane == ids[:, 1:2]
    hits = (jnp.where(hit0, 1.0, 0.0) + jnp.where(hit1, 1.0, 0.0)).astype(BF16)
    lower = (lax.broadcasted_iota(I32, (n, n), 0) >= lax.broadcasted_iota(I32, (n, n), 1)).astype(BF16)
    seen = jnp.dot(lower, hits, preferred_element_type=F32) + cnt_ref[...]
    r0 = jnp.sum(jnp.where(hit0, seen, 0.0), axis=-1, keepdims=True) - 1.0
    r1 = jnp.sum(jnp.where(hit1, seen, 0.0), axis=-1, keepdims=True) - 1.0
    rank_ref[...] = jnp.where(lane == 0, r0, jnp.where(lane == 1, r1, 0.0)).astype(I32)
    cnt_ref[...] = seen[n - 1:n, :]


def rank_slots(ids):
    t = ids.shape[0]
    tm = 512
    return pl.pallas_call(
        _rank_kernel,
        grid=(t // tm,),
        in_specs=[pl.BlockSpec((tm, LANES), lambda i: (i, 0))],
        out_specs=[pl.BlockSpec((tm, LANES), lambda i: (i, 0)), pl.BlockSpec((1, LANES), lambda i: (0, 0))],
        out_shape=[jax.ShapeDtypeStruct((t, LANES), I32), jax.ShapeDtypeStruct((1, LANES), F32)],
        compiler_params=_params("arbitrary"),
        name="rank_slots",
    )(ids)


def _expert_kernel(be_ref, na_ref, first_ref, next_ref, grp_ref, tok_ref, x_ref, wg_hbm, wu_hbm, wd_hbm, o_ref,
                   buf_ref, sg_ref, su_ref, sd_ref, wg_ref, wu_ref, wd_ref, sem, wsem, *, layer):
    blk = pl.program_id(0)
    rows = buf_ref.shape[1]
    n_active = na_ref[0]

    @pl.when(blk == 0)
    def _():
        buf_ref[...] = jnp.zeros_like(buf_ref)

    def weight_copies(e):
        return (pltpu.make_async_copy(wg_hbm.at[layer, e], sg_ref, wsem.at[0]),
                pltpu.make_async_copy(wu_hbm.at[layer, e], su_ref, wsem.at[1]),
                pltpu.make_async_copy(wd_hbm.at[layer, e], sd_ref, wsem.at[2]))

    def row_copy(b, r):
        slot = b % 2
        tok = tok_ref[b * rows + r]
        return pltpu.make_async_copy(x_ref.at[pl.ds(tok, 1), :], buf_ref.at[slot, pl.ds(r, 1), :], sem.at[slot])

    def gather(b):
        def start(g, c):
            for j in range(GATHER_UNROLL):
                row_copy(b, g * GATHER_UNROLL + j).start()
            return c

        lax.fori_loop(0, grp_ref[b], start, 0)

    @pl.when(jnp.logical_and(blk == 0, n_active > 0))
    def _():
        for cp in weight_copies(be_ref[0]):
            cp.start(priority=1)
        gather(blk)

    @pl.when(blk + 1 < n_active)
    def _():
        gather(blk + 1)

    @pl.when(jnp.logical_and(blk < n_active, first_ref[blk] == 1))
    def _():
        for cp in weight_copies(be_ref[blk]):
            cp.wait()
        wg_ref[...] = sg_ref[...].astype(BF16)
        wu_ref[...] = su_ref[...].astype(BF16)
        wd_ref[...] = sd_ref[...].astype(BF16)

        @pl.when(next_ref[blk] >= 0)
        def _():
            for cp in weight_copies(next_ref[blk]):
                cp.start(priority=1)

    @pl.when(blk < n_active)
    def _():
        def wait(g, c):
            for j in range(GATHER_UNROLL):
                row_copy(blk, g * GATHER_UNROLL + j).wait()
            return c

        lax.fori_loop(0, grp_ref[blk], wait, 0)
        x = buf_ref[blk % 2].astype(BF16)
        gate = jnp.dot(x, wg_ref[...], preferred_element_type=F32)
        up = jnp.dot(x, wu_ref[...], preferred_element_type=F32)
        act = (gate * jax.nn.sigmoid(gate) * up).astype(BF16)
        o_ref[...] = jnp.dot(act, wd_ref[...], preferred_element_type=F32)

    @pl.when(blk >= na_ref[0])
    def _():
        o_ref[...] = jnp.zeros_like(o_ref)


def run_experts(block_expert, n_active, block_rows, row_token, x, w_gate, w_up, w_down, layer):
    n_blocks = block_expert.shape[0]
    _, _, d, ff = w_gate.shape
    rows = MOE_ROWS
    idx = jnp.arange(n_blocks, dtype=I32)
    first = jnp.concatenate([jnp.ones((1,), I32), (block_expert[1:] != block_expert[:-1]).astype(I32)])
    later = (idx[None, :] > idx[:, None]) & (block_expert[None, :] != block_expert[:, None]) & (idx[None, :] < n_active[0])
    nxt_blk = jnp.min(jnp.where(later, idx[None, :], n_blocks), axis=1)
    nxt = jnp.where(nxt_blk < n_blocks, block_expert[jnp.minimum(nxt_blk, n_blocks - 1)], -1).astype(I32)
    groups = (-(-block_rows // GATHER_UNROLL)).astype(I32)
    hbm = pl.BlockSpec(memory_space=pl.ANY)
    return pl.pallas_call(
        functools.partial(_expert_kernel, layer=layer),
        grid_spec=pltpu.PrefetchScalarGridSpec(
            num_scalar_prefetch=6,
            grid=(n_blocks,),
            in_specs=[hbm, hbm, hbm, hbm],
            out_specs=pl.BlockSpec((rows, d), lambda i, *_: (i, 0)),
            scratch_shapes=[pltpu.VMEM((2, rows, d), F32),
                            pltpu.VMEM((d, ff), F32), pltpu.VMEM((d, ff), F32), pltpu.VMEM((ff, d), F32),
                            pltpu.VMEM((d, ff), BF16), pltpu.VMEM((d, ff), BF16), pltpu.VMEM((ff, d), BF16),
                            pltpu.SemaphoreType.DMA((2,)), pltpu.SemaphoreType.DMA((3,))],
        ),
        out_shape=jax.ShapeDtypeStruct((n_blocks * rows, d), F32),
        compiler_params=_params("arbitrary"),
        name="run_experts",
    )(block_expert, n_active, first, nxt, groups, row_token, x, w_gate, w_up, w_down)


def _combine_kernel(dest_ref, h_ref, wt_ref, p_ref, gp_ref, wpg_ref, wpp_ref, gf_ref, y_ref, o_ref, buf_ref, sem):
    i = pl.program_id(0)
    tm = h_ref.shape[0]

    def row_copy(t, r, c):
        slot = t % 2
        src = dest_ref[(t * tm + r) * TOP_K + c]
        return pltpu.make_async_copy(y_ref.at[pl.ds(src, 1), :], buf_ref.at[slot, c, pl.ds(r, 1), :], sem.at[slot, c])

    def gather(t):
        def start(r, carry):
            for c in range(TOP_K):
                row_copy(t, r, c).start()
            return carry

        lax.fori_loop(0, tm, start, 0, unroll=GATHER_UNROLL)

    @pl.when(i == 0)
    def _():
        gather(i)

    @pl.when(i + 1 < pl.num_programs(0))
    def _():
        gather(i + 1)

    def wait(r, carry):
        for c in range(TOP_K):
            row_copy(i, r, c).wait()
        return carry

    lax.fori_loop(0, tm, wait, 0, unroll=GATHER_UNROLL)
    wt = wt_ref[...]
    slot = i % 2
    h2 = h_ref[...] + (wt[:, 0:1] * buf_ref[slot, 0] + wt[:, 1:2] * buf_ref[slot, 1])
    gate = jax.nn.sigmoid(jnp.dot(_rms(h2, gp_ref[...]).astype(BF16), wpg_ref[...], preferred_element_type=F32))
    ple = jnp.dot(p_ref[...].astype(BF16), wpp_ref[...], preferred_element_type=F32)
    o_ref[...] = _rms(h2 + gate * ple, gf_ref[...])


def combine_and_finish(dest, h1, wts, p, g_ple, w_ple_gate, w_ple_proj, g_final, y_rows):
    t, d = h1.shape
    tm = min(256, t)
    row = lambda n: pl.BlockSpec((tm, n), lambda i, ds: (i, 0))
    full = lambda a: pl.BlockSpec(a.shape, lambda i, ds: (0, 0))
    gp, gf = g_ple.reshape(1, d), g_final.reshape(1, d)
    return pl.pallas_call(
        _combine_kernel,
        grid_spec=pltpu.PrefetchScalarGridSpec(
            num_scalar_prefetch=1,
            grid=(t // tm,),
            in_specs=[row(d), row(LANES), row(p.shape[1]), full(gp), full(w_ple_gate), full(w_ple_proj), full(gf),
                      pl.BlockSpec(memory_space=pl.ANY)],
            out_specs=row(d),
            scratch_shapes=[pltpu.VMEM((2, TOP_K, tm, d), F32), pltpu.SemaphoreType.DMA((2, TOP_K))],
        ),
        out_shape=jax.ShapeDtypeStruct((t, d), F32),
        compiler_params=_params("arbitrary"),
        name="combine_and_finish",
    )(dest, h1, wts, p, gp, w_ple_gate, w_ple_proj, gf, y_rows)


def _segments(d):
    qk = d // 2
    sizes = dict(gq=qk, gk=qk, gv=d, gr=d, ga=GLA_RANK, fq=d, fk=d, fv=d, ff=d // FOX_DH, mg=d, mf=d)
    start, acc = {}, 0
    for name, size in sizes.items():
        start[name] = acc
        acc += size
    return sizes, start


def kernel(x_prompt, x_sample, cache_fox_k, cache_fox_v, cache_fox_logf, state_gla, page_table,
           p_prompt, p_sample, g_mix, w_in, w_gla_a2, b_gla_a, b_fox_f, g_gla_out, w_o_gla, w_o_fox, w_out,
           g_ffn, w_route_group, b_route_group, w_route_expert, b_route_expert, w_exp_gate, w_exp_up,
           w_exp_down, g_ple, w_ple_gate, w_ple_proj, g_final):
    depth = w_in.shape[0]
    bp, sp, d = x_prompt.shape
    bs, ss, _ = x_sample.shape
    assert ss == 1, "the sample group carries one new token per sequence"
    fox_heads = d // FOX_DH
    n_phys, page_size = cache_fox_k.shape[1:3]
    tp, ts = bp * sp, bs * ss
    sizes, start = _segments(d)
    main = ("gq", "gk", "gv", "gr", "fq", "mg", "mf", "fk", "fv")
    off, acc = {}, 0
    for name in main:
        off[name] = acc
        acc += sizes[name]

    hp = x_prompt.reshape(tp, d)
    hs = x_sample.reshape(ts, d)
    outs = {name: [] for name in ("kp", "vp", "lfp", "sp", "ks", "vs", "lfs", "ss")}
    for i in range(depth):
        w = w_in[i]
        seg = lambda name: w[:, start[name]:start[name] + sizes[name]]
        w_main = jnp.concatenate([seg(n) for n in main], axis=1).astype(BF16)
        w_small = jnp.pad(jnp.concatenate([seg("ga"), seg("ff")], axis=1),
                          ((0, 0), (0, LANES - GLA_RANK - fox_heads))).astype(BF16)
        w_a2_pad = jnp.pad(w_gla_a2[i], ((0, LANES - GLA_RANK), (0, 0))).astype(BF16)
        w_router = jnp.pad(jnp.concatenate([w_route_group[i], w_route_expert[i]], axis=1),
                           ((0, 0), (0, LANES - N_GROUPS - N_EXPERTS))).astype(BF16)
        b_router = jnp.pad(jnp.concatenate([b_route_group[i], b_route_expert[i]]),
                           (0, LANES - N_GROUPS - N_EXPERTS))
        wog, wof, wo = w_o_gla[i].astype(BF16), w_o_fox[i].astype(BF16), w_out[i].astype(BF16)

        def forget(z_small, batch, seq):
            f_t = z_small[:, GLA_RANK:GLA_RANK + fox_heads].reshape(batch, seq, fox_heads).transpose(0, 2, 1)
            return fox_log_forget(f_t, b_fox_f[i])

        us = rms_cast(hs, g_mix[i])
        zs, zss = matmul(us, w_main), matmul(us, w_small, tn=LANES)
        las = gla_log_decay(zss, w_a2_pad, b_gla_a[i])
        fks, fvs = zs[:, off["fk"]:off["fk"] + d], zs[:, off["fv"]:off["fv"] + d]
        lfs_t, _ = forget(zss, 1, ts)
        lfs = lfs_t[0].T
        ogs, s_new = gla_sample(zs, las, g_gla_out[i], state_gla[i], off)
        bias = fox_sample_bias(page_table, lfs.reshape(bs, fox_heads, 1),
                               cache_fox_logf[i].transpose(0, 2, 1))
        n_pages = page_table.shape[1]
        bias = bias.transpose(0, 1, 3, 2).reshape(bs, n_pages // FOX_PAGES_PER_STEP, FOX_PAGES_PER_STEP,
                                                  page_size * fox_heads)
        heads = lambda a: a.reshape(bs, fox_heads, FOX_DH)
        decode = (page_table, heads(zs[:, off["fq"]:off["fq"] + d]), heads(fks), heads(fvs), bias,
                  cache_fox_k, cache_fox_v, i)

        up = rms_cast(hp, g_mix[i])
        zsp = matmul(up, w_small, tn=LANES)
        if projection_fits_decode(page_table, up, w_main):
            ofs, zp = fox_sample(*decode, proj=(up, w_main))
        else:
            ofs, zp = fox_sample(*decode), matmul(up, w_main)
        ofs = ofs.reshape(bs, d)
        lf_t, c_t = forget(zsp, bp, sp)
        ogp, s_fin = gla_prompt(zp, zsp, w_a2_pad, b_gla_a[i], g_gla_out[i], bp, sp, off)
        ofp, fkp, fvp = fox_prompt(zp, c_t.reshape(bp * fox_heads, 1, sp), bp, sp, off, fox_heads)
        mixp = merge_branches(ogp, ofp, wog, wof, zp, off["mg"], off["mf"])
        h1p, hn_all, idp, wtp = out_proj_and_route(mixp, wo, hp, g_ffn[i], w_router, b_router,
                                                   jnp.zeros((tp + ts, d), F32))
        outs["kp"].append(fkp.reshape(bp, sp, fox_heads, FOX_DH))
        outs["vp"].append(fvp.reshape(bp, sp, fox_heads, FOX_DH))
        outs["lfp"].append(lf_t.transpose(0, 2, 1))
        outs["sp"].append(s_fin)

        mixs = merge_branches(ogs, ofs, wog, wof, zs, off["mg"], off["mf"])
        h1s, hn_all, ids, wts = out_proj_and_route(mixs, wo, hs, g_ffn[i], w_router, b_router, hn_all,
                                                   row_offset=tp)
        outs["ks"].append(fks.reshape(bs, ss, fox_heads, FOX_DH))
        outs["vs"].append(fvs.reshape(bs, ss, fox_heads, FOX_DH))
        outs["lfs"].append(lfs.reshape(bs, ss, fox_heads))
        outs["ss"].append(s_new)

        t_all = tp + ts
        n_slots = t_all * TOP_K
        t_pad = -(-t_all // 512) * 512
        ids_all = jnp.concatenate([idp, ids], axis=0)
        ids_pad = jnp.pad(ids_all, ((0, t_pad - t_all), (0, 0)), constant_values=LANES - 1)
        rank, counts = rank_slots(ids_pad)
        counts = counts[0, :N_EXPERTS].astype(I32)
        padded = -(-counts // MOE_ROWS) * MOE_ROWS
        ends = jnp.cumsum(padded)
        experts = ids_all[:, :TOP_K]
        dest = (ends - padded)[experts] + rank[:t_all, :TOP_K]
        n_blocks = -(-n_slots // MOE_ROWS) + N_EXPERTS
        n_active = (ends[-1] // MOE_ROWS).reshape(1)
        block_start = jnp.arange(n_blocks, dtype=I32) * MOE_ROWS
        block_expert = jnp.searchsorted(ends, jnp.minimum(block_start, ends[-1] - 1), side="right").astype(I32)
        tokens = jnp.broadcast_to(jnp.arange(t_all, dtype=I32)[:, None], (t_all, TOP_K))
        row_token = jnp.zeros((n_blocks * MOE_ROWS,), I32).at[dest.reshape(-1)].set(tokens.reshape(-1))
        block_rows = jnp.clip(counts[block_expert] - (block_start - (ends - padded)[block_expert]), 0, MOE_ROWS)
        y_rows = run_experts(block_expert, n_active.astype(I32), block_rows, row_token, hn_all,
                             w_exp_gate, w_exp_up, w_exp_down, i)

        last = i == depth - 1
        assert last, "the final norm is fused into the last layer's finishing kernel"
        fin = (g_ple[i], w_ple_gate[i].astype(BF16), w_ple_proj[i].astype(BF16), g_final, y_rows)
        hp = combine_and_finish(dest[:tp].reshape(-1), h1p, wtp, p_prompt[i].reshape(tp, -1), *fin)
        hs = combine_and_finish(dest[tp:].reshape(-1), h1s, wts, p_sample[i].reshape(ts, -1), *fin)

    stack = lambda name: jnp.stack(outs[name])
    return (hp.reshape(bp, sp, d), hs.reshape(bs, ss, d),
            stack("kp"), stack("vp"), stack("lfp"), stack("sp"),
            stack("ks"), stack("vs"), stack("lfs"), stack("ss"))
```

```python
import functools

import jax
import jax.numpy as jnp
from jax import lax
from jax.experimental import pallas as pl
from jax.experimental.pallas import tpu as pltpu

F32 = jnp.float32
BF16 = jnp.bfloat16
I32 = jnp.int32
HIGHEST = lax.Precision.HIGHEST
NEG_INF = float("-inf")
LOG2_E = 1.4426950408889634

RMS_EPS = 1e-6
GLA_HEADS = 4
GLA_RANK = 16
GLA_TAU = 16.0
GLA_CHUNK = 64
GLA_SUB = 16
FOX_DH = 128
N_GROUPS = 4
EXPERTS_PER_GROUP = 8
N_EXPERTS = N_GROUPS * EXPERTS_PER_GROUP
TOP_K = 2

LANES = 128
MOE_ROWS = 256
GATHER_UNROLL = 8
VMEM_LIMIT = 56 * 2**20


def _params(*sem):
    return pltpu.CompilerParams(dimension_semantics=sem, vmem_limit_bytes=VMEM_LIMIT)


def _log_sigmoid(x):
    return jnp.minimum(x, 0.0) - jnp.log1p(jnp.exp(-jnp.abs(x)))


def _rms(x, g):
    return x * lax.rsqrt(jnp.mean(x * x, axis=-1, keepdims=True) + RMS_EPS) * g


def _nt_dot(a, b):
    return lax.dot_general(a, b, (((1,), (1,)), ((), ())), preferred_element_type=F32)


def _tn_dot(a, b):
    return lax.dot_general(a, b, (((0,), (0,)), ((), ())), preferred_element_type=F32)


def _row_to_col(row):
    n = row.shape[1]
    eye = lax.broadcasted_iota(I32, (n, n), 0) == lax.broadcasted_iota(I32, (n, n), 1)
    return jnp.sum(jnp.where(eye, row, 0.0), axis=1, keepdims=True)


def _rms_cast_kernel(x_ref, g_ref, o_ref):
    o_ref[...] = _rms(x_ref[...], g_ref[...]).astype(o_ref.dtype)


def rms_cast(x, g):
    t, d = x.shape
    tm = min(512, t)
    return pl.pallas_call(
        _rms_cast_kernel,
        grid=(t // tm,),
        in_specs=[pl.BlockSpec((tm, d), lambda i: (i, 0)), pl.BlockSpec((1, d), lambda i: (0, 0))],
        out_specs=pl.BlockSpec((tm, d), lambda i: (i, 0)),
        out_shape=jax.ShapeDtypeStruct((t, d), BF16),
        compiler_params=_params("parallel"),
        name="rms_cast",
    )(x, g.reshape(1, d))


def _mm_kernel(a_ref, w_ref, o_ref):
    o_ref[...] = jnp.dot(a_ref[...], w_ref[...], preferred_element_type=F32).astype(o_ref.dtype)


def matmul(a, w, out_dtype=F32, tm=512, tn=1024):
    t, k = a.shape
    n = w.shape[1]
    tm, tn = min(tm, t), min(tn, n)
    return pl.pallas_call(
        _mm_kernel,
        grid=(n // tn, t // tm),
        in_specs=[pl.BlockSpec((tm, k), lambda j, i: (i, 0)), pl.BlockSpec((k, tn), lambda j, i: (0, j))],
        out_specs=pl.BlockSpec((tm, tn), lambda j, i: (i, j)),
        out_shape=jax.ShapeDtypeStruct((t, n), out_dtype),
        compiler_params=_params("parallel", "parallel"),
        name="matmul",
    )(a, w)


def _la_kernel(z_ref, w_ref, b_ref, o_ref):
    pre = jnp.dot(z_ref[...].astype(BF16), w_ref[...], preferred_element_type=F32) + b_ref[...]
    o_ref[...] = _log_sigmoid(pre) / GLA_TAU


def gla_log_decay(z_small, w_a2_pad, b_a):
    t = z_small.shape[0]
    n = w_a2_pad.shape[1]
    tm = min(512, t)
    return pl.pallas_call(
        _la_kernel,
        grid=(t // tm,),
        in_specs=[pl.BlockSpec((tm, LANES), lambda i: (i, 0)),
                  pl.BlockSpec((LANES, n), lambda i: (0, 0)),
                  pl.BlockSpec((1, n), lambda i: (0, 0))],
        out_specs=pl.BlockSpec((tm, n), lambda i: (i, 0)),
        out_shape=jax.ShapeDtypeStruct((t, n), F32),
        compiler_params=_params("parallel"),
        name="gla_log_decay",
    )(z_small, w_a2_pad, b_a.reshape(1, n))


def _logf_kernel(f_ref, b_ref, lf_ref, c_ref, *, blk):
    lf = _log_sigmoid(f_ref[...] + b_ref[...])
    lf_ref[...] = lf
    s = lf.shape[1]
    upper = (lax.broadcasted_iota(I32, (blk, blk), 0) <= lax.broadcasted_iota(I32, (blk, blk), 1)).astype(F32)
    carry = jnp.zeros((lf.shape[0], 1), F32)
    for j in range(s // blk):
        cs = jnp.dot(lf[:, j * blk:(j + 1) * blk], upper, precision=HIGHEST, preferred_element_type=F32) + carry
        c_ref[:, j * blk:(j + 1) * blk] = cs
        carry = cs[:, blk - 1:blk]


def fox_log_forget(f_t, b_f):
    b, h, s = f_t.shape
    blk = min(256, s)
    spec = pl.BlockSpec((None, h, s), lambda i: (i, 0, 0))
    return pl.pallas_call(
        functools.partial(_logf_kernel, blk=blk),
        grid=(b,),
        in_specs=[spec, pl.BlockSpec((h, 1), lambda i: (0, 0))],
        out_specs=[spec, spec],
        out_shape=[jax.ShapeDtypeStruct((b, h, s), F32)] * 2,
        compiler_params=_params("parallel"),
        name="fox_log_forget",
    )(f_t, b_f.reshape(h, 1))


def _gla_prompt_kernel(q_ref, k_ref, v_ref, r_ref, zs_ref, wa_ref, ba_ref, g_ref, o_ref, s_ref, *, rows, heads):
    @pl.when(pl.program_id(2) == 0)
    def _():
        s_ref[...] = jnp.zeros_like(s_ref)

    c, sub = GLA_CHUNK, GLA_SUB
    nsub = c // sub
    dk = q_ref.shape[1] // heads
    dv = v_ref.shape[1] // heads
    scale = dk ** -0.5
    row = lax.broadcasted_iota(I32, (c, c), 0)
    col = lax.broadcasted_iota(I32, (c, c), 1)
    lower = (row >= col).astype(F32)
    col_sub = lax.broadcasted_iota(I32, (sub, c), 1)
    row_in_sub = lax.broadcasted_iota(I32, (c, 1), 0) % sub

    def sub_rows(x, s):
        return jnp.concatenate(
            [jnp.broadcast_to(x[i * sub + s:i * sub + s + 1, :], (sub, dk)) for i in range(nsub)], axis=0)

    def one_head(sl, hh, la):
        kcols = slice(hh * dk, (hh + 1) * dk)
        vcols = slice(hh * dv, (hh + 1) * dv)
        q = q_ref[sl, kcols] * scale
        k = k_ref[sl, kcols]
        vb = v_ref[sl, vcols].astype(BF16)
        b = jnp.dot(lower, la[:, kcols], precision=HIGHEST, preferred_element_type=F32)
        state = s_ref[hh]

        o = jnp.dot((q * jnp.exp(b)).astype(BF16), state.astype(BF16), preferred_element_type=F32)

        a = jnp.zeros((c, c), F32)
        for s in range(sub):
            e = jnp.exp(jnp.where(row_in_sub >= s, b - sub_rows(b, s), NEG_INF))
            w = jnp.sum(q * e * sub_rows(k, s), axis=-1, keepdims=True)
            a = jnp.where(col == (row // sub) * sub + s, w, a)
        blocks = [a[:sub]]
        for i in range(1, nsub):
            ri = slice(i * sub, (i + 1) * sub)
            edge = b[i * sub - 1:i * sub, :]
            qs = (q[ri] * jnp.exp(b[ri] - edge)).astype(BF16)
            ks = (k * jnp.exp(jnp.minimum(edge - b, 0.0))).astype(BF16)
            blocks.append(jnp.where(col_sub < i * sub, _nt_dot(qs, ks), a[ri]))
        a = jnp.concatenate(blocks, axis=0)
        o = o + jnp.dot(a.astype(BF16), vb, preferred_element_type=F32)

        r = r_ref[sl, vcols]
        o_ref[sl, vcols] = (_rms(o, g_ref[...]) * (r * jax.nn.sigmoid(r))).astype(o_ref.dtype)

        last = b[c - 1:c, :]
        kd = (k * jnp.exp(last - b)).astype(BF16)
        s_ref[hh] = _row_to_col(jnp.exp(last)) * state + _tn_dot(kd, vb)

    def chunk(ci, carry):
        sl = pl.ds(pl.multiple_of(ci * c, c), c)
        pre = jnp.dot(zs_ref[sl, :].astype(BF16), wa_ref[...], preferred_element_type=F32) + ba_ref[...]
        la = _log_sigmoid(pre) / GLA_TAU
        for hh in range(heads):
            one_head(sl, hh, la)
        return carry

    lax.fori_loop(0, rows // c, chunk, 0)


GLA_HEADS_PER_STEP = 4


def gla_prompt(z, z_small, w_a2_pad, b_a, g_out, batch, seq, off):
    t = z.shape[0]
    hp = GLA_HEADS_PER_STEP
    dk = w_a2_pad.shape[1] // GLA_HEADS
    dv = g_out.shape[0]
    wk, wv = hp * dk, hp * dv
    rows = min(512, seq)
    nt = seq // rows
    kern = functools.partial(_gla_prompt_kernel, rows=rows, heads=hp)
    rmap = lambda b, h, i: b * nt + i
    return pl.pallas_call(
        kern,
        grid=(batch, GLA_HEADS // hp, nt),
        in_specs=[
            pl.BlockSpec((rows, wk), lambda b, h, i: (rmap(b, h, i), off["gq"] // wk + h)),
            pl.BlockSpec((rows, wk), lambda b, h, i: (rmap(b, h, i), off["gk"] // wk + h)),
            pl.BlockSpec((rows, wv), lambda b, h, i: (rmap(b, h, i), off["gv"] // wv + h)),
            pl.BlockSpec((rows, wv), lambda b, h, i: (rmap(b, h, i), off["gr"] // wv + h)),
            pl.BlockSpec((rows, LANES), lambda b, h, i: (rmap(b, h, i), 0)),
            pl.BlockSpec((LANES, wk), lambda b, h, i: (0, h)),
            pl.BlockSpec((1, wk), lambda b, h, i: (0, h)),
            pl.BlockSpec((1, dv), lambda b, h, i: (0, 0)),
        ],
        out_specs=[
            pl.BlockSpec((rows, wv), lambda b, h, i: (rmap(b, h, i), h)),
            pl.BlockSpec((None, hp, dk, dv), lambda b, h, i: (b, h, 0, 0)),
        ],
        out_shape=[jax.ShapeDtypeStruct((t, GLA_HEADS * dv), BF16),
                   jax.ShapeDtypeStruct((batch, GLA_HEADS, dk, dv), F32)],
        compiler_params=_params("parallel", "parallel", "arbitrary"),
        name="gla_prompt",
    )(z, z, z, z, z_small, w_a2_pad, b_a.reshape(1, -1), g_out.reshape(1, dv))


def _gla_sample_kernel(q_ref, k_ref, v_ref, r_ref, la_ref, g_ref, s_ref, o_ref, sn_ref):
    h, dk, dv = s_ref.shape
    for j in range(h):
        kc = slice(j * dk, (j + 1) * dk)
        vc = slice(j * dv, (j + 1) * dv)
        q = q_ref[:, kc] * dk ** -0.5
        k = k_ref[:, kc]
        v = v_ref[:, vc]
        decay = jnp.exp(la_ref[:, kc])
        state = s_ref[j]
        sn_ref[j] = _row_to_col(decay) * state + _row_to_col(k) * v
        qe = _row_to_col((q * decay).astype(BF16).astype(F32))
        o = jnp.sum(qe * state.astype(BF16).astype(F32), axis=0, keepdims=True)
        o = o + jnp.sum(q * k, axis=1, keepdims=True) * v
        r = r_ref[:, vc]
        o_ref[:, vc] = (_rms(o, g_ref[...]) * (r * jax.nn.sigmoid(r))).astype(o_ref.dtype)


def gla_sample(z, la, g_out, state, off):
    b = z.shape[0]
    _, h, dk, dv = state.shape
    wk, wv = h * dk, h * dv
    z3 = z.reshape(b, 1, z.shape[1])
    la3 = la.reshape(b, 1, la.shape[1])
    og, sn = pl.pallas_call(
        _gla_sample_kernel,
        grid=(b,),
        in_specs=[
            pl.BlockSpec((None, 1, wk), lambda i: (i, 0, off["gq"] // wk)),
            pl.BlockSpec((None, 1, wk), lambda i: (i, 0, off["gk"] // wk)),
            pl.BlockSpec((None, 1, wv), lambda i: (i, 0, off["gv"] // wv)),
            pl.BlockSpec((None, 1, wv), lambda i: (i, 0, off["gr"] // wv)),
            pl.BlockSpec((None, 1, wk), lambda i: (i, 0, 0)),
            pl.BlockSpec((1, dv), lambda i: (0, 0)),
            pl.BlockSpec((None, h, dk, dv), lambda i: (i, 0, 0, 0)),
        ],
        out_specs=[
            pl.BlockSpec((None, 1, wv), lambda i: (i, 0, 0)),
            pl.BlockSpec((None, h, dk, dv), lambda i: (i, 0, 0, 0)),
        ],
        out_shape=[jax.ShapeDtypeStruct((b, 1, wv), BF16), jax.ShapeDtypeStruct(state.shape, F32)],
        compiler_params=_params("parallel"),
        name="gla_sample",
    )(z3, z3, z3, z3, la3, g_out.reshape(1, dv), state)
    return og.reshape(b, wv), sn


def _fox_prompt_kernel(q_ref, k_ref, v_ref, c_ref, o_ref, ko_ref, vo_ref, kb_ref, vb_ref, *, blk):
    s, dh = q_ref.shape
    ko_ref[...] = k_ref[...]
    vo_ref[...] = v_ref[...]
    kb_ref[...] = k_ref[...].astype(BF16)
    vb_ref[...] = v_ref[...].astype(BF16)
    causal = lax.broadcasted_iota(I32, (blk, blk), 0) >= lax.broadcasted_iota(I32, (blk, blk), 1)
    c2 = c_ref[...] * LOG2_E
    for i in range(s // blk):
        qs = slice(i * blk, (i + 1) * blk)
        q = (q_ref[qs, :] * (dh ** -0.5 * LOG2_E)).astype(BF16)
        m = jnp.full((blk, 1), NEG_INF, F32)
        l = jnp.zeros((blk, 1), F32)
        acc = jnp.zeros((blk, dh), F32)
        for j in range(i + 1):
            ks = slice(j * blk, (j + 1) * blk)
            logit = _nt_dot(q, kb_ref[ks, :]) - c2[:, ks]
            if j == i:
                logit = jnp.where(causal, logit, NEG_INF)
            m_new = jnp.maximum(m, jnp.max(logit, axis=-1, keepdims=True))
            alpha = jnp.exp2(m - m_new)
            p = jnp.exp2(logit - m_new)
            l = alpha * l + jnp.sum(p, axis=-1, keepdims=True)
            acc = alpha * acc + jnp.dot(p.astype(BF16), vb_ref[ks, :], preferred_element_type=F32)
            m = m_new
        o_ref[qs, :] = (acc / l).astype(o_ref.dtype)


def fox_prompt(z, c_t, batch, seq, off, h):
    t = z.shape[0]
    w = h * FOX_DH
    blk = min(512, seq)
    head = pl.BlockSpec((seq, FOX_DH), lambda b, j: (b, j))
    return pl.pallas_call(
        functools.partial(_fox_prompt_kernel, blk=blk),
        grid=(batch, h),
        in_specs=[
            pl.BlockSpec((seq, FOX_DH), lambda b, j: (b, off["fq"] // FOX_DH + j)),
            pl.BlockSpec((seq, FOX_DH), lambda b, j: (b, off["fk"] // FOX_DH + j)),
            pl.BlockSpec((seq, FOX_DH), lambda b, j: (b, off["fv"] // FOX_DH + j)),
            pl.BlockSpec((None, 1, seq), lambda b, j: (b * h + j, 0, 0)),
        ],
        out_specs=[head, head, head],
        out_shape=[jax.ShapeDtypeStruct((t, w), BF16), jax.ShapeDtypeStruct((t, w), F32),
                   jax.ShapeDtypeStruct((t, w), F32)],
        scratch_shapes=[pltpu.VMEM((seq, FOX_DH), BF16)] * 2,
        compiler_params=_params("parallel", "parallel"),
        name="fox_prompt",
    )(z, z, z, c_t)


def _fox_bias_kernel(pt_ref, lfn_ref, pool_ref, o_ref, buf_ref, sem):
    b = pl.program_id(0)
    npg, h, ps = buf_ref.shape

    def page_copy(p):
        return pltpu.make_async_copy(pool_ref.at[pt_ref[b, p]], buf_ref.at[p], sem)

    for p in range(npg):
        page_copy(p).start()
    for p in range(npg):
        page_copy(p).wait()

    x = buf_ref[...]
    after = (lax.broadcasted_iota(I32, (ps, ps), 0) >= lax.broadcasted_iota(I32, (ps, ps), 1)).astype(F32)
    suffix = jnp.dot(x.reshape(npg * h, ps), after, precision=HIGHEST,
                     preferred_element_type=F32).reshape(npg, h, ps)
    run = lfn_ref[...]
    for p in range(npg - 1, -1, -1):
        o_ref[p] = suffix[p] - x[p] + run
        run = run + suffix[p][:, 0:1]


def fox_sample_bias(page_table, lf_new, pool_lf_t):
    b, npg = page_table.shape
    _, h, ps = pool_lf_t.shape
    return pl.pallas_call(
        _fox_bias_kernel,
        grid_spec=pltpu.PrefetchScalarGridSpec(
            num_scalar_prefetch=1,
            grid=(b,),
            in_specs=[pl.BlockSpec((None, h, 1), lambda i, pt: (i, 0, 0)),
                      pl.BlockSpec(memory_space=pl.ANY)],
            out_specs=pl.BlockSpec((None, npg, h, ps), lambda i, pt: (i, 0, 0, 0)),
            scratch_shapes=[pltpu.VMEM((npg, h, ps), F32), pltpu.SemaphoreType.DMA(())],
        ),
        out_shape=jax.ShapeDtypeStruct((b, npg, h, ps), F32),
        compiler_params=_params("arbitrary"),
        name="fox_sample_bias",
    )(page_table, lf_new, pool_lf_t)


def _fox_sample_kernel(pt_ref, q_ref, kn_ref, vn_ref, bias_ref, *refs, pages, with_matmul):
    k_refs, v_refs = refs[:pages], refs[pages:2 * pages]
    if with_matmul:
        a_ref, w_ref, o_ref, z_ref, m_ref, l_ref, acc_ref = refs[2 * pages:]
        z_ref[...] = jnp.dot(a_ref[...], w_ref[...], preferred_element_type=F32)
    else:
        o_ref, m_ref, l_ref, acc_ref = refs[2 * pages:]
    p = pl.program_id(1)
    h, dh = q_ref.shape
    n = k_refs[0].shape[0] * h
    q = q_ref[...] * dh ** -0.5

    @pl.when(p == 0)
    def _():
        m_ref[...] = jnp.broadcast_to(jnp.sum(q * kn_ref[...], axis=1, keepdims=True), m_ref.shape)
        l_ref[...] = jnp.ones_like(l_ref)
        acc_ref[...] = vn_ref[...]

    own = lax.broadcasted_iota(I32, (h, n), 1) % h == lax.broadcasted_iota(I32, (h, n), 0)
    qb = q.astype(BF16)
    logits = [jnp.where(own, _nt_dot(qb, k_refs[r][...].reshape(n, dh).astype(BF16)) + bias_ref[r:r + 1, :], NEG_INF)
              for r in range(pages)]
    m = m_ref[:, 0:1]
    m_new = m
    for lg in logits:
        m_new = jnp.maximum(m_new, jnp.max(lg, axis=-1, keepdims=True))
    alpha = jnp.exp(m - m_new)
    l = alpha * l_ref[:, 0:1]
    acc = alpha * acc_ref[...]
    for r in range(pages):
        pr = jnp.exp(logits[r] - m_new)
        l = l + jnp.sum(pr, axis=-1, keepdims=True)
        acc = acc + jnp.dot(pr.astype(BF16), v_refs[r][...].reshape(n, dh).astype(BF16), preferred_element_type=F32)
    m_ref[...] = jnp.broadcast_to(m_new, m_ref.shape)
    l_ref[...] = jnp.broadcast_to(l, l_ref.shape)
    acc_ref[...] = acc

    @pl.when(p == pl.num_programs(1) - 1)
    def _():
        o_ref[...] = (acc / l).astype(o_ref.dtype)


FOX_PAGES_PER_STEP = 8


PROJ_TILE = (512, 1024)


def projection_fits_decode(page_table, a, w):
    tm, tn = PROJ_TILE
    steps = page_table.shape[0] * (page_table.shape[1] // FOX_PAGES_PER_STEP)
    return a.shape[0] % tm == 0 and w.shape[1] % tn == 0 and (a.shape[0] // tm) * (w.shape[1] // tn) == steps


def fox_sample(page_table, q, k_new, v_new, bias, cache_k, cache_v, layer, proj=None):
    b, npg = page_table.shape
    _, _, ps, h, dh = cache_k.shape
    g = FOX_PAGES_PER_STEP
    row = pl.BlockSpec((None, h, dh), lambda i, p, pt: (i, 0, 0))
    extra_in, extra_args, out_specs = [], [], row
    out_shape = jax.ShapeDtypeStruct((b, h, dh), BF16)
    if proj is not None:
        a, w = proj
        tm, tn = PROJ_TILE
        k = a.shape[1]
        nr = a.shape[0] // tm
        tile = lambda i, p: i * (npg // g) + p
        extra_in = [pl.BlockSpec((tm, k), lambda i, p, pt: (tile(i, p) % nr, 0)),
                    pl.BlockSpec((k, tn), lambda i, p, pt: (0, tile(i, p) // nr))]
        extra_args = [a, w]
        out_specs = [row, pl.BlockSpec((tm, tn), lambda i, p, pt: (tile(i, p) % nr, tile(i, p) // nr))]
        out_shape = [out_shape, jax.ShapeDtypeStruct((a.shape[0], w.shape[1]), F32)]

    def page(r):
        return pl.BlockSpec((None, None, ps, h, dh), lambda i, p, pt: (layer, pt[i, p * g + r], 0, 0, 0))

    pages = [page(r) for r in range(g)]
    return pl.pallas_call(
        functools.partial(_fox_sample_kernel, pages=g, with_matmul=proj is not None),
        grid_spec=pltpu.PrefetchScalarGridSpec(
            num_scalar_prefetch=1,
            grid=(b, npg // g),
            in_specs=[row, row, row,
                      pl.BlockSpec((None, None, g, ps * h), lambda i, p, pt: (i, p, 0, 0)),
                      *pages, *pages, *extra_in],
            out_specs=out_specs,
            scratch_shapes=[pltpu.VMEM((h, LANES), F32), pltpu.VMEM((h, LANES), F32), pltpu.VMEM((h, dh), F32)],
        ),
        out_shape=out_shape,
        compiler_params=_params("arbitrary", "arbitrary"),
        name="fox_sample",
    )(page_table, q, k_new, v_new, bias, *([cache_k] * g), *([cache_v] * g), *extra_args)


def _merge_kernel(og_ref, of_ref, wg_ref, wf_ref, mg_ref, mf_ref, o_ref):
    yg = jnp.dot(og_ref[...], wg_ref[...], preferred_element_type=F32)
    yf = jnp.dot(of_ref[...], wf_ref[...], preferred_element_type=F32)
    o_ref[...] = (jax.nn.sigmoid(mg_ref[...]) * yg + jax.nn.sigmoid(mf_ref[...]) * yf).astype(o_ref.dtype)


def merge_branches(og, of, w_o_gla, w_o_fox, z, mg_off, mf_off):
    t, k = og.shape
    n = w_o_gla.shape[1]
    tm, tn = min(512, t), min(1024, n)
    return pl.pallas_call(
        _merge_kernel,
        grid=(n // tn, t // tm),
        in_specs=[pl.BlockSpec((tm, k), lambda j, i: (i, 0)),
                  pl.BlockSpec((tm, k), lambda j, i: (i, 0)),
                  pl.BlockSpec((k, tn), lambda j, i: (0, j)),
                  pl.BlockSpec((k, tn), lambda j, i: (0, j)),
                  pl.BlockSpec((tm, tn), lambda j, i: (i, mg_off // tn + j)),
                  pl.BlockSpec((tm, tn), lambda j, i: (i, mf_off // tn + j))],
        out_specs=pl.BlockSpec((tm, tn), lambda j, i: (i, j)),
        out_shape=jax.ShapeDtypeStruct((t, n), BF16),
        compiler_params=_params("parallel", "parallel"),
        name="merge_branches",
    )(og, of, w_o_gla, w_o_fox, z, z)


def _route_kernel(mix_ref, w_ref, x_ref, g_ref, wr_ref, br_ref, *refs):
    h_ref, hn_ref, id_ref, wt_ref = refs[-4:]
    h1 = x_ref[...] + jnp.dot(mix_ref[...], w_ref[...], preferred_element_type=F32)
    h_ref[...] = h1
    hn = _rms(h1, g_ref[...])
    hn_ref[...] = hn
    logits = jnp.dot(hn.astype(BF16), wr_ref[...], preferred_element_type=F32) + br_ref[...]
    lane = lax.broadcasted_iota(I32, logits.shape, 1)
    big = jnp.int32(2 ** 30)

    def top(vals):
        best = jnp.max(vals, axis=-1, keepdims=True)
        return best, jnp.min(jnp.where(vals == best, lane, big), axis=-1, keepdims=True)

    grp = jnp.where(lane < N_GROUPS, logits, NEG_INF)
    g_max, g_idx = top(grp)
    g_w = 1.0 / jnp.sum(jnp.exp(grp - g_max), axis=-1, keepdims=True)
    lo = N_GROUPS + g_idx * EXPERTS_PER_GROUP
    mine = jnp.where(lane >= lo, jnp.where(lane < lo + EXPERTS_PER_GROUP, logits, NEG_INF), NEG_INF)
    e1, i1 = top(mine)
    e2, i2 = top(jnp.where(lane == i1, NEG_INF, mine))
    z = jnp.sum(jnp.exp(mine - e1), axis=-1, keepdims=True)
    p1 = 1.0 / z
    p2 = jnp.exp(e2 - e1) / z
    w1 = g_w * p1 / (p1 + p2)
    w2 = g_w * p2 / (p1 + p2)
    id_ref[...] = jnp.where(lane == 0, i1 - N_GROUPS, jnp.where(lane == 1, i2 - N_GROUPS, 0))
    wt_ref[...] = jnp.where(lane == 0, w1, jnp.where(lane == 1, w2, 0.0))


def out_proj_and_route(mix, w_out, x, g_ffn, w_router, b_router, normed, row_offset=0):
    t, d = x.shape
    tm = min(256, t)
    assert row_offset % tm == 0
    row = lambda n: pl.BlockSpec((tm, n), lambda i: (i, 0))
    full = lambda a: pl.BlockSpec(a.shape, lambda i: (0, 0))
    g2, b2 = g_ffn.reshape(1, d), b_router.reshape(1, LANES)
    args = [mix, w_out, x, g2, w_router, b2, normed]
    in_specs = [row(d), full(w_out), row(d), full(g2), full(w_router), full(b2), pl.BlockSpec(memory_space=pl.ANY)]
    return pl.pallas_call(
        _route_kernel,
        grid=(t // tm,),
        in_specs=in_specs,
        out_specs=[row(d), pl.BlockSpec((tm, d), lambda i: (row_offset // tm + i, 0)), row(LANES), row(LANES)],
        out_shape=[jax.ShapeDtypeStruct((t, d), F32), jax.ShapeDtypeStruct(normed.shape, F32),
                   jax.ShapeDtypeStruct((t, LANES), I32), jax.ShapeDtypeStruct((t, LANES), F32)],
        input_output_aliases={6: 1},
        compiler_params=_params("parallel"),
        name="out_proj_and_route",
    )(*args)


def _rank_kernel(id_ref, rank_ref, cnt_ref):
    @pl.when(pl.program_id(0) == 0)
    def _():
        cnt_ref[...] = jnp.zeros_like(cnt_ref)

    ids = id_ref[...]
    n = ids.shape[0]
    lane = lax.broadcasted_iota(I32, ids.shape, 1)
    hit0 = lane == ids[:, 0:1]
    hit1 = lane == ids[:, 1:2]
    hits = (jnp.where(hit0, 1.0, 0.0) + jnp.where(hit1, 1.0, 0.0)).astype(BF16)
    lower = (lax.broadcasted_iota(I32, (n, n), 0) >= lax.broadcasted_iota(I32, (n, n), 1)).astype(BF16)
    seen = jnp.dot(lower, hits, preferred_element_type=F32) + cnt_ref[...]
    r0 = jnp.sum(jnp.where(hit0, seen, 0.0), axis=-1, keepdims=True) - 1.0
    r1 = jnp.sum(jnp.where(hit1, seen, 0.0), axis=-1, keepdims=True) - 1.0
    rank_ref[...] = jnp.where(lane == 0, r0, jnp.where(lane == 1, r1, 0.0)).astype(I32)
    cnt_ref[...] = seen[n - 1:n, :]


def rank_slots(ids):
    t = ids.shape[0]
    tm = 512
    return pl.pallas_call(
        _rank_kernel,
        grid=(t // tm,),
        in_specs=[pl.BlockSpec((tm, LANES), lambda i: (i, 0))],
        out_specs=[pl.BlockSpec((tm, LANES), lambda i: (i, 0)), pl.BlockSpec((1, LANES), lambda i: (0, 0))],
        out_shape=[jax.ShapeDtypeStruct((t, LANES), I32), jax.ShapeDtypeStruct((1, LANES), F32)],
        compiler_params=_params("arbitrary"),
        name="rank_slots",
    )(ids)


def _expert_kernel(be_ref, na_ref, first_ref, next_ref, grp_ref, tok_ref, x_ref, wg_hbm, wu_hbm, wd_hbm, o_ref,
                   buf_ref, sg_ref, su_ref, sd_ref, wg_ref, wu_ref, wd_ref, sem, wsem, *, layer):
    blk = pl.program_id(0)
    rows = buf_ref.shape[1]
    n_active = na_ref[0]

    @pl.when(blk == 0)
    def _():
        buf_ref[...] = jnp.zeros_like(buf_ref)

    def weight_copies(e):
        return (pltpu.make_async_copy(wg_hbm.at[layer, e], sg_ref, wsem.at[0]),
                pltpu.make_async_copy(wu_hbm.at[layer, e], su_ref, wsem.at[1]),
                pltpu.make_async_copy(wd_hbm.at[layer, e], sd_ref, wsem.at[2]))

    def row_copy(b, r):
        slot = b % 2
        tok = tok_ref[b * rows + r]
        return pltpu.make_async_copy(x_ref.at[pl.ds(tok, 1), :], buf_ref.at[slot, pl.ds(r, 1), :], sem.at[slot])

    def gather(b):
        def start(g, c):
            for j in range(GATHER_UNROLL):
                row_copy(b, g * GATHER_UNROLL + j).start()
            return c

        lax.fori_loop(0, grp_ref[b], start, 0)

    @pl.when(jnp.logical_and(blk == 0, n_active > 0))
    def _():
        for cp in weight_copies(be_ref[0]):
            cp.start(priority=1)
        gather(blk)

    @pl.when(blk + 1 < n_active)
    def _():
        gather(blk + 1)

    @pl.when(jnp.logical_and(blk < n_active, first_ref[blk] == 1))
    def _():
        for cp in weight_copies(be_ref[blk]):
            cp.wait()
        wg_ref[...] = sg_ref[...].astype(BF16)
        wu_ref[...] = su_ref[...].astype(BF16)
        wd_ref[...] = sd_ref[...].astype(BF16)

        @pl.when(next_ref[blk] >= 0)
        def _():
            for cp in weight_copies(next_ref[blk]):
                cp.start(priority=1)

    @pl.when(blk < n_active)
    def _():
        def wait(g, c):
            for j in range(GATHER_UNROLL):
                row_copy(blk, g * GATHER_UNROLL + j).wait()
            return c

        lax.fori_loop(0, grp_ref[blk], wait, 0)
        x = buf_ref[blk % 2].astype(BF16)
        gate = jnp.dot(x, wg_ref[...], preferred_element_type=F32)
        up = jnp.dot(x, wu_ref[...], preferred_element_type=F32)
        act = (gate * jax.nn.sigmoid(gate) * up).astype(BF16)
        o_ref[...] = jnp.dot(act, wd_ref[...], preferred_element_type=F32)

    @pl.when(blk >= na_ref[0])
    def _():
        o_ref[...] = jnp.zeros_like(o_ref)


def run_experts(block_expert, n_active, block_rows, row_token, x, w_gate, w_up, w_down, layer):
    n_blocks = block_expert.shape[0]
    _, _, d, ff = w_gate.shape
    rows = MOE_ROWS
    idx = jnp.arange(n_blocks, dtype=I32)
    first = jnp.concatenate([jnp.ones((1,), I32), (block_expert[1:] != block_expert[:-1]).astype(I32)])
    later = (idx[None, :] > idx[:, None]) & (block_expert[None, :] != block_expert[:, None]) & (idx[None, :] < n_active[0])
    nxt_blk = jnp.min(jnp.where(later, idx[None, :], n_blocks), axis=1)
    nxt = jnp.where(nxt_blk < n_blocks, block_expert[jnp.minimum(nxt_blk, n_blocks - 1)], -1).astype(I32)
    groups = (-(-block_rows // GATHER_UNROLL)).astype(I32)
    hbm = pl.BlockSpec(memory_space=pl.ANY)
    return pl.pallas_call(
        functools.partial(_expert_kernel, layer=layer),
        grid_spec=pltpu.PrefetchScalarGridSpec(
            num_scalar_prefetch=6,
            grid=(n_blocks,),
            in_specs=[hbm, hbm, hbm, hbm],
            out_specs=pl.BlockSpec((rows, d), lambda i, *_: (i, 0)),
            scratch_shapes=[pltpu.VMEM((2, rows, d), F32),
                            pltpu.VMEM((d, ff), F32), pltpu.VMEM((d, ff), F32), pltpu.VMEM((ff, d), F32),
                            pltpu.VMEM((d, ff), BF16), pltpu.VMEM((d, ff), BF16), pltpu.VMEM((ff, d), BF16),
                            pltpu.SemaphoreType.DMA((2,)), pltpu.SemaphoreType.DMA((3,))],
        ),
        out_shape=jax.ShapeDtypeStruct((n_blocks * rows, d), F32),
        compiler_params=_params("arbitrary"),
        name="run_experts",
    )(block_expert, n_active, first, nxt, groups, row_token, x, w_gate, w_up, w_down)


def _combine_kernel(dest_ref, h_ref, wt_ref, p_ref, gp_ref, wpg_ref, wpp_ref, gf_ref, y_ref, o_ref, buf_ref, sem):
    i = pl.program_id(0)
    tm = h_ref.shape[0]

    def row_copy(t, r, c):
        slot = t % 2
        src = dest_ref[(t * tm + r) * TOP_K + c]
        return pltpu.make_async_copy(y_ref.at[pl.ds(src, 1), :], buf_ref.at[slot, c, pl.ds(r, 1), :], sem.at[slot, c])

    def gather(t):
        def start(r, carry):
            for c in range(TOP_K):
                row_copy(t, r, c).start()
            return carry

        lax.fori_loop(0, tm, start, 0, unroll=GATHER_UNROLL)

    @pl.when(i == 0)
    def _():
        gather(i)

    @pl.when(i + 1 < pl.num_programs(0))
    def _():
        gather(i + 1)

    def wait(r, carry):
        for c in range(TOP_K):
            row_copy(i, r, c).wait()
        return carry

    lax.fori_loop(0, tm, wait, 0, unroll=GATHER_UNROLL)
    wt = wt_ref[...]
    slot = i % 2
    h2 = h_ref[...] + (wt[:, 0:1] * buf_ref[slot, 0] + wt[:, 1:2] * buf_ref[slot, 1])
    gate = jax.nn.sigmoid(jnp.dot(_rms(h2, gp_ref[...]).astype(BF16), wpg_ref[...], preferred_element_type=F32))
    ple = jnp.dot(p_ref[...].astype(BF16), wpp_ref[...], preferred_element_type=F32)
    o_ref[...] = _rms(h2 + gate * ple, gf_ref[...])


def combine_and_finish(dest, h1, wts, p, g_ple, w_ple_gate, w_ple_proj, g_final, y_rows):
    t, d = h1.shape
    tm = min(256, t)
    row = lambda n: pl.BlockSpec((tm, n), lambda i, ds: (i, 0))
    full = lambda a: pl.BlockSpec(a.shape, lambda i, ds: (0, 0))
    gp, gf = g_ple.reshape(1, d), g_final.reshape(1, d)
    return pl.pallas_call(
        _combine_kernel,
        grid_spec=pltpu.PrefetchScalarGridSpec(
            num_scalar_prefetch=1,
            grid=(t // tm,),
            in_specs=[row(d), row(LANES), row(p.shape[1]), full(gp), full(w_ple_gate), full(w_ple_proj), full(gf),
                      pl.BlockSpec(memory_space=pl.ANY)],
            out_specs=row(d),
            scratch_shapes=[pltpu.VMEM((2, TOP_K, tm, d), F32), pltpu.SemaphoreType.DMA((2, TOP_K))],
        ),
        out_shape=jax.ShapeDtypeStruct((t, d), F32),
        compiler_params=_params("arbitrary"),
        name="combine_and_finish",
    )(dest, h1, wts, p, gp, w_ple_gate, w_ple_proj, gf, y_rows)


def _segments(d):
    qk = d // 2
    sizes = dict(gq=qk, gk=qk, gv=d, gr=d, ga=GLA_RANK, fq=d, fk=d, fv=d, ff=d // FOX_DH, mg=d, mf=d)
    start, acc = {}, 0
    for name, size in sizes.items():
        start[name] = acc
        acc += size
    return sizes, start


def kernel(x_prompt, x_sample, cache_fox_k, cache_fox_v, cache_fox_logf, state_gla, page_table,
           p_prompt, p_sample, g_mix, w_in, w_gla_a2, b_gla_a, b_fox_f, g_gla_out, w_o_gla, w_o_fox, w_out,
           g_ffn, w_route_group, b_route_group, w_route_expert, b_route_expert, w_exp_gate, w_exp_up,
           w_exp_down, g_ple, w_ple_gate, w_ple_proj, g_final):
    depth = w_in.shape[0]
    bp, sp, d = x_prompt.shape
    bs, ss, _ = x_sample.shape
    assert ss == 1, "the sample group carries one new token per sequence"
    fox_heads = d // FOX_DH
    n_phys, page_size = cache_fox_k.shape[1:3]
    tp, ts = bp * sp, bs * ss
    sizes, start = _segments(d)
    main = ("gq", "gk", "gv", "gr", "fq", "mg", "mf", "fk", "fv")
    off, acc = {}, 0
    for name in main:
        off[name] = acc
        acc += sizes[name]

    hp = x_prompt.reshape(tp, d)
    hs = x_sample.reshape(ts, d)
    outs = {name: [] for name in ("kp", "vp", "lfp", "sp", "ks", "vs", "lfs", "ss")}
    for i in range(depth):
        w = w_in[i]
        seg = lambda name: w[:, start[name]:start[name] + sizes[name]]
        w_main = jnp.concatenate([seg(n) for n in main], axis=1).astype(BF16)
        w_small = jnp.pad(jnp.concatenate([seg("ga"), seg("ff")], axis=1),
                          ((0, 0), (0, LANES - GLA_RANK - fox_heads))).astype(BF16)
        w_a2_pad = jnp.pad(w_gla_a2[i], ((0, LANES - GLA_RANK), (0, 0))).astype(BF16)
        w_router = jnp.pad(jnp.concatenate([w_route_group[i], w_route_expert[i]], axis=1),
                           ((0, 0), (0, LANES - N_GROUPS - N_EXPERTS))).astype(BF16)
        b_router = jnp.pad(jnp.concatenate([b_route_group[i], b_route_expert[i]]),
                           (0, LANES - N_GROUPS - N_EXPERTS))
        wog, wof, wo = w_o_gla[i].astype(BF16), w_o_fox[i].astype(BF16), w_out[i].astype(BF16)

        def forget(z_small, batch, seq):
            f_t = z_small[:, GLA_RANK:GLA_RANK + fox_heads].reshape(batch, seq, fox_heads).transpose(0, 2, 1)
            return fox_log_forget(f_t, b_fox_f[i])

        us = rms_cast(hs, g_mix[i])
        zs, zss = matmul(us, w_main), matmul(us, w_small, tn=LANES)
        las = gla_log_decay(zss, w_a2_pad, b_gla_a[i])
        fks, fvs = zs[:, off["fk"]:off["fk"] + d], zs[:, off["fv"]:off["fv"] + d]
        lfs_t, _ = forget(zss, 1, ts)
        lfs = lfs_t[0].T
        bias = fox_sample_bias(page_table, lfs.reshape(bs, fox_heads, 1),
                               cache_fox_logf[i].transpose(0, 2, 1))
        n_pages = page_table.shape[1]
        bias = bias.transpose(0, 1, 3, 2).reshape(bs, n_pages // FOX_PAGES_PER_STEP, FOX_PAGES_PER_STEP,
                                                  page_size * fox_heads)
        heads = lambda a: a.reshape(bs, fox_heads, FOX_DH)
        decode = (page_table, heads(zs[:, off["fq"]:off["fq"] + d]), heads(fks), heads(fvs), bias,
                  cache_fox_k, cache_fox_v, i)

        up = rms_cast(hp, g_mix[i])
        zsp = matmul(up, w_small, tn=LANES)
        if projection_fits_decode(page_table, up, w_main):
            ofs, zp = fox_sample(*decode, proj=(up, w_main))
        else:
            ofs, zp = fox_sample(*decode), matmul(up, w_main)
        ofs = ofs.reshape(bs, d)
        lf_t, c_t = forget(zsp, bp, sp)
        ogp, s_fin = gla_prompt(zp, zsp, w_a2_pad, b_gla_a[i], g_gla_out[i], bp, sp, off)
        ofp, fkp, fvp = fox_prompt(zp, c_t.reshape(bp * fox_heads, 1, sp), bp, sp, off, fox_heads)
        mixp = merge_branches(ogp, ofp, wog, wof, zp, off["mg"], off["mf"])
        h1p, hn_all, idp, wtp = out_proj_and_route(mixp, wo, hp, g_ffn[i], w_router, b_router,
                                                   jnp.zeros((tp + ts, d), F32))
        outs["kp"].append(fkp.reshape(bp, sp, fox_heads, FOX_DH))
        outs["vp"].append(fvp.reshape(bp, sp, fox_heads, FOX_DH))
        outs["lfp"].append(lf_t.transpose(0, 2, 1))
        outs["sp"].append(s_fin)

        ogs, s_new = gla_sample(zs, las, g_gla_out[i], state_gla[i], off)
        mixs = merge_branches(ogs, ofs, wog, wof, zs, off["mg"], off["mf"])
        h1s, hn_all, ids, wts = out_proj_and_route(mixs, wo, hs, g_ffn[i], w_router, b_router, hn_all,
                                                   row_offset=tp)
        outs["ks"].append(fks.reshape(bs, ss, fox_heads, FOX_DH))
        outs["vs"].append(fvs.reshape(bs, ss, fox_heads, FOX_DH))
        outs["lfs"].append(lfs.reshape(bs, ss, fox_heads))
        outs["ss"].append(s_new)

        t_all = tp + ts
        n_slots = t_all * TOP_K
        t_pad = -(-t_all // 512) * 512
        ids_all = jnp.concatenate([idp, ids], axis=0)
        ids_pad = jnp.pad(ids_all, ((0, t_pad - t_all), (0, 0)), constant_values=LANES - 1)
        rank, counts = rank_slots(ids_pad)
        counts = counts[0, :N_EXPERTS].astype(I32)
        padded = -(-counts // MOE_ROWS) * MOE_ROWS
        ends = jnp.cumsum(padded)
        experts = ids_all[:, :TOP_K]
        dest = (ends - padded)[experts] + rank[:t_all, :TOP_K]
        n_blocks = -(-n_slots // MOE_ROWS) + N_EXPERTS
        n_active = (ends[-1] // MOE_ROWS).reshape(1)
        block_start = jnp.arange(n_blocks, dtype=I32) * MOE_ROWS
        block_expert = jnp.searchsorted(ends, jnp.minimum(block_start, ends[-1] - 1), side="right").astype(I32)
        tokens = jnp.broadcast_to(jnp.arange(t_all, dtype=I32)[:, None], (t_all, TOP_K))
        row_token = jnp.zeros((n_blocks * MOE_ROWS,), I32).at[dest.reshape(-1)].set(tokens.reshape(-1))
        block_rows = jnp.clip(counts[block_expert] - (block_start - (ends - padded)[block_expert]), 0, MOE_ROWS)
        y_rows = run_experts(block_expert, n_active.astype(I32), block_rows, row_token, hn_all,
                             w_exp_gate, w_exp_up, w_exp_down, i)

        last = i == depth - 1
        assert last, "the final norm is fused into the last layer's finishing kernel"
        fin = (g_ple[i], w_ple_gate[i].astype(BF16), w_ple_proj[i].astype(BF16), g_final, y_rows)
        hp = combine_and_finish(dest[:tp].reshape(-1), h1p, wtp, p_prompt[i].reshape(tp, -1), *fin)
        hs = combine_and_finish(dest[tp:].reshape(-1), h1s, wts, p_sample[i].reshape(ts, -1), *fin)

    stack = lambda name: jnp.stack(outs[name])
    return (hp.reshape(bp, sp, d), hs.reshape(bs, ss, d),
            stack("kp"), stack("vp"), stack("lfp"), stack("sp"),
            stack("ks"), stack("vs"), stack("lfs"), stack("ss"))
```

```python
import functools

import jax
import jax.numpy as jnp
from jax import lax
from jax.experimental import pallas as pl
from jax.experimental.pallas import tpu as pltpu

F32 = jnp.float32
BF16 = jnp.bfloat16
I32 = jnp.int32
HIGHEST = lax.Precision.HIGHEST
NEG_INF = float("-inf")

RMS_EPS = 1e-6
GLA_HEADS = 4
GLA_RANK = 16
GLA_TAU = 16.0
GLA_CHUNK = 64
GLA_SUB = 16
FOX_DH = 128
N_GROUPS = 4
EXPERTS_PER_GROUP = 8
N_EXPERTS = N_GROUPS * EXPERTS_PER_GROUP
TOP_K = 2

LANES = 128
MOE_ROWS = 256
GATHER_UNROLL = 8
VMEM_LIMIT = 56 * 2**20


def _params(*sem):
    return pltpu.CompilerParams(dimension_semantics=sem, vmem_limit_bytes=VMEM_LIMIT)


def _log_sigmoid(x):
    return jnp.minimum(x, 0.0) - jnp.log1p(jnp.exp(-jnp.abs(x)))


def _rms(x, g):
    return x * lax.rsqrt(jnp.mean(x * x, axis=-1, keepdims=True) + RMS_EPS) * g


def _nt_dot(a, b):
    return lax.dot_general(a, b, (((1,), (1,)), ((), ())), preferred_element_type=F32)


def _tn_dot(a, b):
    return lax.dot_general(a, b, (((0,), (0,)), ((), ())), preferred_element_type=F32)


def _row_to_col(row):
    n = row.shape[1]
    eye = lax.broadcasted_iota(I32, (n, n), 0) == lax.broadcasted_iota(I32, (n, n), 1)
    return jnp.sum(jnp.where(eye, row, 0.0), axis=1, keepdims=True)


def _rms_cast_kernel(x_ref, g_ref, o_ref):
    o_ref[...] = _rms(x_ref[...], g_ref[...]).astype(o_ref.dtype)


def rms_cast(x, g):
    t, d = x.shape
    tm = min(512, t)
    return pl.pallas_call(
        _rms_cast_kernel,
        grid=(t // tm,),
        in_specs=[pl.BlockSpec((tm, d), lambda i: (i, 0)), pl.BlockSpec((1, d), lambda i: (0, 0))],
        out_specs=pl.BlockSpec((tm, d), lambda i: (i, 0)),
        out_shape=jax.ShapeDtypeStruct((t, d), BF16),
        compiler_params=_params("parallel"),
        name="rms_cast",
    )(x, g.reshape(1, d))


def _mm_kernel(a_ref, w_ref, o_ref):
    o_ref[...] = jnp.dot(a_ref[...], w_ref[...], preferred_element_type=F32).astype(o_ref.dtype)


def matmul(a, w, out_dtype=F32, tm=512, tn=1024):
    t, k = a.shape
    n = w.shape[1]
    tm, tn = min(tm, t), min(tn, n)
    return pl.pallas_call(
        _mm_kernel,
        grid=(n // tn, t // tm),
        in_specs=[pl.BlockSpec((tm, k), lambda j, i: (i, 0)), pl.BlockSpec((k, tn), lambda j, i: (0, j))],
        out_specs=pl.BlockSpec((tm, tn), lambda j, i: (i, j)),
        out_shape=jax.ShapeDtypeStruct((t, n), out_dtype),
        compiler_params=_params("parallel", "parallel"),
        name="matmul",
    )(a, w)


def _la_kernel(z_ref, w_ref, b_ref, o_ref):
    pre = jnp.dot(z_ref[...].astype(BF16), w_ref[...], preferred_element_type=F32) + b_ref[...]
    o_ref[...] = _log_sigmoid(pre) / GLA_TAU


def gla_log_decay(z_small, w_a2_pad, b_a):
    t = z_small.shape[0]
    n = w_a2_pad.shape[1]
    tm = min(512, t)
    return pl.pallas_call(
        _la_kernel,
        grid=(t // tm,),
        in_specs=[pl.BlockSpec((tm, LANES), lambda i: (i, 0)),
                  pl.BlockSpec((LANES, n), lambda i: (0, 0)),
                  pl.BlockSpec((1, n), lambda i: (0, 0))],
        out_specs=pl.BlockSpec((tm, n), lambda i: (i, 0)),
        out_shape=jax.ShapeDtypeStruct((t, n), F32),
        compiler_params=_params("parallel"),
        name="gla_log_decay",
    )(z_small, w_a2_pad, b_a.reshape(1, n))


def _logf_kernel(f_ref, b_ref, lf_ref, c_ref, *, blk):
    lf = _log_sigmoid(f_ref[...] + b_ref[...])
    lf_ref[...] = lf
    s = lf.shape[1]
    upper = (lax.broadcasted_iota(I32, (blk, blk), 0) <= lax.broadcasted_iota(I32, (blk, blk), 1)).astype(F32)
    carry = jnp.zeros((lf.shape[0], 1), F32)
    for j in range(s // blk):
        cs = jnp.dot(lf[:, j * blk:(j + 1) * blk], upper, precision=HIGHEST, preferred_element_type=F32) + carry
        c_ref[:, j * blk:(j + 1) * blk] = cs
        carry = cs[:, blk - 1:blk]


def fox_log_forget(f_t, b_f):
    b, h, s = f_t.shape
    blk = min(256, s)
    spec = pl.BlockSpec((None, h, s), lambda i: (i, 0, 0))
    return pl.pallas_call(
        functools.partial(_logf_kernel, blk=blk),
        grid=(b,),
        in_specs=[spec, pl.BlockSpec((h, 1), lambda i: (0, 0))],
        out_specs=[spec, spec],
        out_shape=[jax.ShapeDtypeStruct((b, h, s), F32)] * 2,
        compiler_params=_params("parallel"),
        name="fox_log_forget",
    )(f_t, b_f.reshape(h, 1))


def _gla_prompt_kernel(q_ref, k_ref, v_ref, r_ref, zs_ref, wa_ref, ba_ref, g_ref, o_ref, s_ref, *, rows, heads):
    @pl.when(pl.program_id(2) == 0)
    def _():
        s_ref[...] = jnp.zeros_like(s_ref)

    c, sub = GLA_CHUNK, GLA_SUB
    nsub = c // sub
    dk = q_ref.shape[1] // heads
    dv = v_ref.shape[1] // heads
    scale = dk ** -0.5
    row = lax.broadcasted_iota(I32, (c, c), 0)
    col = lax.broadcasted_iota(I32, (c, c), 1)
    lower = (row >= col).astype(F32)
    col_sub = lax.broadcasted_iota(I32, (sub, c), 1)
    row_in_sub = lax.broadcasted_iota(I32, (c, 1), 0) % sub

    def sub_rows(x, s):
        return jnp.concatenate(
            [jnp.broadcast_to(x[i * sub + s:i * sub + s + 1, :], (sub, dk)) for i in range(nsub)], axis=0)

    def one_head(sl, hh, la):
        kcols = slice(hh * dk, (hh + 1) * dk)
        vcols = slice(hh * dv, (hh + 1) * dv)
        q = q_ref[sl, kcols] * scale
        k = k_ref[sl, kcols]
        vb = v_ref[sl, vcols].astype(BF16)
        b = jnp.dot(lower, la[:, kcols], precision=HIGHEST, preferred_element_type=F32)
        state = s_ref[hh]

        o = jnp.dot((q * jnp.exp(b)).astype(BF16), state.astype(BF16), preferred_element_type=F32)

        a = jnp.zeros((c, c), F32)
        for s in range(sub):
            e = jnp.exp(jnp.where(row_in_sub >= s, b - sub_rows(b, s), NEG_INF))
            w = jnp.sum(q * e * sub_rows(k, s), axis=-1, keepdims=True)
            a = jnp.where(col == (row // sub) * sub + s, w, a)
        blocks = [a[:sub]]
        for i in range(1, nsub):
            ri = slice(i * sub, (i + 1) * sub)
            edge = b[i * sub - 1:i * sub, :]
            qs = (q[ri] * jnp.exp(b[ri] - edge)).astype(BF16)
            ks = (k * jnp.exp(jnp.minimum(edge - b, 0.0))).astype(BF16)
            blocks.append(jnp.where(col_sub < i * sub, _nt_dot(qs, ks), a[ri]))
        a = jnp.concatenate(blocks, axis=0)
        o = o + jnp.dot(a.astype(BF16), vb, preferred_element_type=F32)

        r = r_ref[sl, vcols]
        o_ref[sl, vcols] = (_rms(o, g_ref[...]) * (r * jax.nn.sigmoid(r))).astype(o_ref.dtype)

        last = b[c - 1:c, :]
        kd = (k * jnp.exp(last - b)).astype(BF16)
        s_ref[hh] = _row_to_col(jnp.exp(last)) * state + _tn_dot(kd, vb)

    def chunk(ci, carry):
        sl = pl.ds(pl.multiple_of(ci * c, c), c)
        pre = jnp.dot(zs_ref[sl, :].astype(BF16), wa_ref[...], preferred_element_type=F32) + ba_ref[...]
        la = _log_sigmoid(pre) / GLA_TAU
        for hh in range(heads):
            one_head(sl, hh, la)
        return carry

    lax.fori_loop(0, rows // c, chunk, 0)


GLA_HEADS_PER_STEP = 4


def gla_prompt(z, z_small, w_a2_pad, b_a, g_out, batch, seq, off):
    t = z.shape[0]
    hp = GLA_HEADS_PER_STEP
    dk = w_a2_pad.shape[1] // GLA_HEADS
    dv = g_out.shape[0]
    wk, wv = hp * dk, hp * dv
    rows = min(512, seq)
    nt = seq // rows
    kern = functools.partial(_gla_prompt_kernel, rows=rows, heads=hp)
    rmap = lambda b, h, i: b * nt + i
    return pl.pallas_call(
        kern,
        grid=(batch, GLA_HEADS // hp, nt),
        in_specs=[
            pl.BlockSpec((rows, wk), lambda b, h, i: (rmap(b, h, i), off["gq"] // wk + h)),
            pl.BlockSpec((rows, wk), lambda b, h, i: (rmap(b, h, i), off["gk"] // wk + h)),
            pl.BlockSpec((rows, wv), lambda b, h, i: (rmap(b, h, i), off["gv"] // wv + h)),
            pl.BlockSpec((rows, wv), lambda b, h, i: (rmap(b, h, i), off["gr"] // wv + h)),
            pl.BlockSpec((rows, LANES), lambda b, h, i: (rmap(b, h, i), 0)),
            pl.BlockSpec((LANES, wk), lambda b, h, i: (0, h)),
            pl.BlockSpec((1, wk), lambda b, h, i: (0, h)),
            pl.BlockSpec((1, dv), lambda b, h, i: (0, 0)),
        ],
        out_specs=[
            pl.BlockSpec((rows, wv), lambda b, h, i: (rmap(b, h, i), h)),
            pl.BlockSpec((None, hp, dk, dv), lambda b, h, i: (b, h, 0, 0)),
        ],
        out_shape=[jax.ShapeDtypeStruct((t, GLA_HEADS * dv), BF16),
                   jax.ShapeDtypeStruct((batch, GLA_HEADS, dk, dv), F32)],
        compiler_params=_params("parallel", "parallel", "arbitrary"),
        name="gla_prompt",
    )(z, z, z, z, z_small, w_a2_pad, b_a.reshape(1, -1), g_out.reshape(1, dv))


def _gla_sample_kernel(q_ref, k_ref, v_ref, r_ref, la_ref, g_ref, s_ref, o_ref, sn_ref):
    h, dk, dv = s_ref.shape
    for j in range(h):
        kc = slice(j * dk, (j + 1) * dk)
        vc = slice(j * dv, (j + 1) * dv)
        q = q_ref[:, kc] * dk ** -0.5
        k = k_ref[:, kc]
        v = v_ref[:, vc]
        decay = jnp.exp(la_ref[:, kc])
        state = s_ref[j]
        sn_ref[j] = _row_to_col(decay) * state + _row_to_col(k) * v
        qe = _row_to_col((q * decay).astype(BF16).astype(F32))
        o = jnp.sum(qe * state.astype(BF16).astype(F32), axis=0, keepdims=True)
        o = o + jnp.sum(q * k, axis=1, keepdims=True) * v
        r = r_ref[:, vc]
        o_ref[:, vc] = (_rms(o, g_ref[...]) * (r * jax.nn.sigmoid(r))).astype(o_ref.dtype)


def gla_sample(z, la, g_out, state, off):
    b = z.shape[0]
    _, h, dk, dv = state.shape
    wk, wv = h * dk, h * dv
    z3 = z.reshape(b, 1, z.shape[1])
    la3 = la.reshape(b, 1, la.shape[1])
    og, sn = pl.pallas_call(
        _gla_sample_kernel,
        grid=(b,),
        in_specs=[
            pl.BlockSpec((None, 1, wk), lambda i: (i, 0, off["gq"] // wk)),
            pl.BlockSpec((None, 1, wk), lambda i: (i, 0, off["gk"] // wk)),
            pl.BlockSpec((None, 1, wv), lambda i: (i, 0, off["gv"] // wv)),
            pl.BlockSpec((None, 1, wv), lambda i: (i, 0, off["gr"] // wv)),
            pl.BlockSpec((None, 1, wk), lambda i: (i, 0, 0)),
            pl.BlockSpec((1, dv), lambda i: (0, 0)),
            pl.BlockSpec((None, h, dk, dv), lambda i: (i, 0, 0, 0)),
        ],
        out_specs=[
            pl.BlockSpec((None, 1, wv), lambda i: (i, 0, 0)),
            pl.BlockSpec((None, h, dk, dv), lambda i: (i, 0, 0, 0)),
        ],
        out_shape=[jax.ShapeDtypeStruct((b, 1, wv), BF16), jax.ShapeDtypeStruct(state.shape, F32)],
        compiler_params=_params("parallel"),
        name="gla_sample",
    )(z3, z3, z3, z3, la3, g_out.reshape(1, dv), state)
    return og.reshape(b, wv), sn


def _fox_prompt_kernel(q_ref, k_ref, v_ref, c_ref, o_ref, ko_ref, vo_ref, kb_ref, vb_ref, *, blk):
    s, dh = q_ref.shape
    ko_ref[...] = k_ref[...]
    vo_ref[...] = v_ref[...]
    kb_ref[...] = k_ref[...].astype(BF16)
    vb_ref[...] = v_ref[...].astype(BF16)
    causal = lax.broadcasted_iota(I32, (blk, blk), 0) >= lax.broadcasted_iota(I32, (blk, blk), 1)
    for i in range(s // blk):
        qs = slice(i * blk, (i + 1) * blk)
        q = (q_ref[qs, :] * dh ** -0.5).astype(BF16)
        m = jnp.full((blk, 1), NEG_INF, F32)
        l = jnp.zeros((blk, 1), F32)
        acc = jnp.zeros((blk, dh), F32)
        for j in range(i + 1):
            ks = slice(j * blk, (j + 1) * blk)
            logit = _nt_dot(q, kb_ref[ks, :]) - c_ref[:, ks]
            if j == i:
                logit = jnp.where(causal, logit, NEG_INF)
            m_new = jnp.maximum(m, jnp.max(logit, axis=-1, keepdims=True))
            alpha = jnp.exp(m - m_new)
            p = jnp.exp(logit - m_new)
            l = alpha * l + jnp.sum(p, axis=-1, keepdims=True)
            acc = alpha * acc + jnp.dot(p.astype(BF16), vb_ref[ks, :], preferred_element_type=F32)
            m = m_new
        o_ref[qs, :] = (acc / l).astype(o_ref.dtype)


def fox_prompt(z, c_t, batch, seq, off, h):
    t = z.shape[0]
    w = h * FOX_DH
    blk = min(512, seq)
    head = pl.BlockSpec((seq, FOX_DH), lambda b, j: (b, j))
    return pl.pallas_call(
        functools.partial(_fox_prompt_kernel, blk=blk),
        grid=(batch, h),
        in_specs=[
            pl.BlockSpec((seq, FOX_DH), lambda b, j: (b, off["fq"] // FOX_DH + j)),
            pl.BlockSpec((seq, FOX_DH), lambda b, j: (b, off["fk"] // FOX_DH + j)),
            pl.BlockSpec((seq, FOX_DH), lambda b, j: (b, off["fv"] // FOX_DH + j)),
            pl.BlockSpec((None, 1, seq), lambda b, j: (b * h + j, 0, 0)),
        ],
        out_specs=[head, head, head],
        out_shape=[jax.ShapeDtypeStruct((t, w), BF16), jax.ShapeDtypeStruct((t, w), F32),
                   jax.ShapeDtypeStruct((t, w), F32)],
        scratch_shapes=[pltpu.VMEM((seq, FOX_DH), BF16)] * 2,
        compiler_params=_params("parallel", "parallel"),
        name="fox_prompt",
    )(z, z, z, c_t)


def _fox_bias_kernel(pt_ref, lfn_ref, pool_ref, o_ref, buf_ref, sem):
    b = pl.program_id(0)
    npg, h, ps = buf_ref.shape

    def page_copy(p):
        return pltpu.make_async_copy(pool_ref.at[pt_ref[b, p]], buf_ref.at[p], sem)

    for p in range(npg):
        page_copy(p).start()
    for p in range(npg):
        page_copy(p).wait()

    x = buf_ref[...]
    after = (lax.broadcasted_iota(I32, (ps, ps), 0) >= lax.broadcasted_iota(I32, (ps, ps), 1)).astype(F32)
    suffix = jnp.dot(x.reshape(npg * h, ps), after, precision=HIGHEST,
                     preferred_element_type=F32).reshape(npg, h, ps)
    run = lfn_ref[...]
    for p in range(npg - 1, -1, -1):
        o_ref[p] = suffix[p] - x[p] + run
        run = run + suffix[p][:, 0:1]


def fox_sample_bias(page_table, lf_new, pool_lf_t):
    b, npg = page_table.shape
    _, h, ps = pool_lf_t.shape
    return pl.pallas_call(
        _fox_bias_kernel,
        grid_spec=pltpu.PrefetchScalarGridSpec(
            num_scalar_prefetch=1,
            grid=(b,),
            in_specs=[pl.BlockSpec((None, h, 1), lambda i, pt: (i, 0, 0)),
                      pl.BlockSpec(memory_space=pl.ANY)],
            out_specs=pl.BlockSpec((None, npg, h, ps), lambda i, pt: (i, 0, 0, 0)),
            scratch_shapes=[pltpu.VMEM((npg, h, ps), F32), pltpu.SemaphoreType.DMA(())],
        ),
        out_shape=jax.ShapeDtypeStruct((b, npg, h, ps), F32),
        compiler_params=_params("arbitrary"),
        name="fox_sample_bias",
    )(page_table, lf_new, pool_lf_t)


def _fox_sample_kernel(pt_ref, q_ref, kn_ref, vn_ref, bias_ref, *refs, pages, with_matmul):
    k_refs, v_refs = refs[:pages], refs[pages:2 * pages]
    if with_matmul:
        a_ref, w_ref, o_ref, z_ref, m_ref, l_ref, acc_ref = refs[2 * pages:]
        z_ref[...] = jnp.dot(a_ref[...], w_ref[...], preferred_element_type=F32)
    else:
        o_ref, m_ref, l_ref, acc_ref = refs[2 * pages:]
    p = pl.program_id(1)
    h, dh = q_ref.shape
    n = k_refs[0].shape[0] * h
    q = q_ref[...] * dh ** -0.5

    @pl.when(p == 0)
    def _():
        m_ref[...] = jnp.broadcast_to(jnp.sum(q * kn_ref[...], axis=1, keepdims=True), m_ref.shape)
        l_ref[...] = jnp.ones_like(l_ref)
        acc_ref[...] = vn_ref[...]

    own = lax.broadcasted_iota(I32, (h, n), 1) % h == lax.broadcasted_iota(I32, (h, n), 0)
    qb = q.astype(BF16)
    logits = [jnp.where(own, _nt_dot(qb, k_refs[r][...].reshape(n, dh).astype(BF16)) + bias_ref[r:r + 1, :], NEG_INF)
              for r in range(pages)]
    m = m_ref[:, 0:1]
    m_new = m
    for lg in logits:
        m_new = jnp.maximum(m_new, jnp.max(lg, axis=-1, keepdims=True))
    alpha = jnp.exp(m - m_new)
    l = alpha * l_ref[:, 0:1]
    acc = alpha * acc_ref[...]
    for r in range(pages):
        pr = jnp.exp(logits[r] - m_new)
        l = l + jnp.sum(pr, axis=-1, keepdims=True)
        acc = acc + jnp.dot(pr.astype(BF16), v_refs[r][...].reshape(n, dh).astype(BF16), preferred_element_type=F32)
    m_ref[...] = jnp.broadcast_to(m_new, m_ref.shape)
    l_ref[...] = jnp.broadcast_to(l, l_ref.shape)
    acc_ref[...] = acc

    @pl.when(p == pl.num_programs(1) - 1)
    def _():
        o_ref[...] = (acc / l).astype(o_ref.dtype)


FOX_PAGES_PER_STEP = 8


PROJ_TILE = (512, 1024)


def projection_fits_decode(page_table, a, w):
    tm, tn = PROJ_TILE
    steps = page_table.shape[0] * (page_table.shape[1] // FOX_PAGES_PER_STEP)
    return a.shape[0] % tm == 0 and w.shape[1] % tn == 0 and (a.shape[0] // tm) * (w.shape[1] // tn) == steps


def fox_sample(page_table, q, k_new, v_new, bias, cache_k, cache_v, layer, proj=None):
    b, npg = page_table.shape
    _, _, ps, h, dh = cache_k.shape
    g = FOX_PAGES_PER_STEP
    row = pl.BlockSpec((None, h, dh), lambda i, p, pt: (i, 0, 0))
    extra_in, extra_args, out_specs = [], [], row
    out_shape = jax.ShapeDtypeStruct((b, h, dh), BF16)
    if proj is not None:
        a, w = proj
        tm, tn = PROJ_TILE
        k = a.shape[1]
        nr = a.shape[0] // tm
        tile = lambda i, p: i * (npg // g) + p
        extra_in = [pl.BlockSpec((tm, k), lambda i, p, pt: (tile(i, p) % nr, 0)),
                    pl.BlockSpec((k, tn), lambda i, p, pt: (0, tile(i, p) // nr))]
        extra_args = [a, w]
        out_specs = [row, pl.BlockSpec((tm, tn), lambda i, p, pt: (tile(i, p) % nr, tile(i, p) // nr))]
        out_shape = [out_shape, jax.ShapeDtypeStruct((a.shape[0], w.shape[1]), F32)]

    def page(r):
        return pl.BlockSpec((None, None, ps, h, dh), lambda i, p, pt: (layer, pt[i, p * g + r], 0, 0, 0))

    pages = [page(r) for r in range(g)]
    return pl.pallas_call(
        functools.partial(_fox_sample_kernel, pages=g, with_matmul=proj is not None),
        grid_spec=pltpu.PrefetchScalarGridSpec(
            num_scalar_prefetch=1,
            grid=(b, npg // g),
            in_specs=[row, row, row,
                      pl.BlockSpec((None, None, g, ps * h), lambda i, p, pt: (i, p, 0, 0)),
                      *pages, *pages, *extra_in],
            out_specs=out_specs,
            scratch_shapes=[pltpu.VMEM((h, LANES), F32), pltpu.VMEM((h, LANES), F32), pltpu.VMEM((h, dh), F32)],
        ),
        out_shape=out_shape,
        compiler_params=_params("arbitrary", "arbitrary"),
        name="fox_sample",
    )(page_table, q, k_new, v_new, bias, *([cache_k] * g), *([cache_v] * g), *extra_args)


def _merge_kernel(og_ref, of_ref, wg_ref, wf_ref, mg_ref, mf_ref, o_ref):
    yg = jnp.dot(og_ref[...], wg_ref[...], preferred_element_type=F32)
    yf = jnp.dot(of_ref[...], wf_ref[...], preferred_element_type=F32)
    o_ref[...] = (jax.nn.sigmoid(mg_ref[...]) * yg + jax.nn.sigmoid(mf_ref[...]) * yf).astype(o_ref.dtype)


def merge_branches(og, of, w_o_gla, w_o_fox, z, mg_off, mf_off):
    t, k = og.shape
    n = w_o_gla.shape[1]
    tm, tn = min(512, t), min(1024, n)
    return pl.pallas_call(
        _merge_kernel,
        grid=(n // tn, t // tm),
        in_specs=[pl.BlockSpec((tm, k), lambda j, i: (i, 0)),
                  pl.BlockSpec((tm, k), lambda j, i: (i, 0)),
                  pl.BlockSpec((k, tn), lambda j, i: (0, j)),
                  pl.BlockSpec((k, tn), lambda j, i: (0, j)),
                  pl.BlockSpec((tm, tn), lambda j, i: (i, mg_off // tn + j)),
                  pl.BlockSpec((tm, tn), lambda j, i: (i, mf_off // tn + j))],
        out_specs=pl.BlockSpec((tm, tn), lambda j, i: (i, j)),
        out_shape=jax.ShapeDtypeStruct((t, n), BF16),
        compiler_params=_params("parallel", "parallel"),
        name="merge_branches",
    )(og, of, w_o_gla, w_o_fox, z, z)


def _route_kernel(mix_ref, w_ref, x_ref, g_ref, wr_ref, br_ref, *refs):
    h_ref, hn_ref, id_ref, wt_ref = refs[-4:]
    h1 = x_ref[...] + jnp.dot(mix_ref[...], w_ref[...], preferred_element_type=F32)
    h_ref[...] = h1
    hn = _rms(h1, g_ref[...])
    hn_ref[...] = hn
    logits = jnp.dot(hn.astype(BF16), wr_ref[...], preferred_element_type=F32) + br_ref[...]
    lane = lax.broadcasted_iota(I32, logits.shape, 1)
    big = jnp.int32(2 ** 30)

    def top(vals):
        best = jnp.max(vals, axis=-1, keepdims=True)
        return best, jnp.min(jnp.where(vals == best, lane, big), axis=-1, keepdims=True)

    grp = jnp.where(lane < N_GROUPS, logits, NEG_INF)
    g_max, g_idx = top(grp)
    g_w = 1.0 / jnp.sum(jnp.exp(grp - g_max), axis=-1, keepdims=True)
    lo = N_GROUPS + g_idx * EXPERTS_PER_GROUP
    mine = jnp.where(lane >= lo, jnp.where(lane < lo + EXPERTS_PER_GROUP, logits, NEG_INF), NEG_INF)
    e1, i1 = top(mine)
    e2, i2 = top(jnp.where(lane == i1, NEG_INF, mine))
    z = jnp.sum(jnp.exp(mine - e1), axis=-1, keepdims=True)
    p1 = 1.0 / z
    p2 = jnp.exp(e2 - e1) / z
    w1 = g_w * p1 / (p1 + p2)
    w2 = g_w * p2 / (p1 + p2)
    id_ref[...] = jnp.where(lane == 0, i1 - N_GROUPS, jnp.where(lane == 1, i2 - N_GROUPS, 0))
    wt_ref[...] = jnp.where(lane == 0, w1, jnp.where(lane == 1, w2, 0.0))


def out_proj_and_route(mix, w_out, x, g_ffn, w_router, b_router, normed, row_offset=0):
    t, d = x.shape
    tm = min(512, t)
    assert row_offset % tm == 0
    row = lambda n: pl.BlockSpec((tm, n), lambda i: (i, 0))
    full = lambda a: pl.BlockSpec(a.shape, lambda i: (0, 0), pipeline_mode=pl.Buffered(1))
    g2, b2 = g_ffn.reshape(1, d), b_router.reshape(1, LANES)
    args = [mix, w_out, x, g2, w_router, b2, normed]
    in_specs = [row(d), full(w_out), row(d), full(g2), full(w_router), full(b2), pl.BlockSpec(memory_space=pl.ANY)]
    return pl.pallas_call(
        _route_kernel,
        grid=(t // tm,),
        in_specs=in_specs,
        out_specs=[row(d), pl.BlockSpec((tm, d), lambda i: (row_offset // tm + i, 0)), row(LANES), row(LANES)],
        out_shape=[jax.ShapeDtypeStruct((t, d), F32), jax.ShapeDtypeStruct(normed.shape, F32),
                   jax.ShapeDtypeStruct((t, LANES), I32), jax.ShapeDtypeStruct((t, LANES), F32)],
        input_output_aliases={6: 1},
        compiler_params=_params("parallel"),
        name="out_proj_and_route",
    )(*args)


def _rank_kernel(id_ref, rank_ref, cnt_ref):
    @pl.when(pl.program_id(0) == 0)
    def _():
        cnt_ref[...] = jnp.zeros_like(cnt_ref)

    ids = id_ref[...]
    n = ids.shape[0]
    lane = lax.broadcasted_iota(I32, ids.shape, 1)
    hit0 = lane == ids[:, 0:1]
    hit1 = lane == ids[:, 1:2]
    hits = (jnp.where(hit0, 1.0, 0.0) + jnp.where(hit1, 1.0, 0.0)).astype(BF16)
    lower = (lax.broadcasted_iota(I32, (n, n), 0) >= lax.broadcasted_iota(I32, (n, n), 1)).astype(BF16)
    seen = jnp.dot(lower, hits, preferred_element_type=F32) + cnt_ref[...]
    r0 = jnp.sum(jnp.where(hit0, seen, 0.0), axis=-1, keepdims=True) - 1.0
    r1 = jnp.sum(jnp.where(hit1, seen, 0.0), axis=-1, keepdims=True) - 1.0
    rank_ref[...] = jnp.where(lane == 0, r0, jnp.where(lane == 1, r1, 0.0)).astype(I32)
    cnt_ref[...] = seen[n - 1:n, :]


def rank_slots(ids):
    t = ids.shape[0]
    tm = 512
    return pl.pallas_call(
        _rank_kernel,
        grid=(t // tm,),
        in_specs=[pl.BlockSpec((tm, LANES), lambda i: (i, 0))],
        out_specs=[pl.BlockSpec((tm, LANES), lambda i: (i, 0)), pl.BlockSpec((1, LANES), lambda i: (0, 0))],
        out_shape=[jax.ShapeDtypeStruct((t, LANES), I32), jax.ShapeDtypeStruct((1, LANES), F32)],
        compiler_params=_params("arbitrary"),
        name="rank_slots",
    )(ids)


def _expert_kernel(be_ref, na_ref, first_ref, next_ref, grp_ref, tok_ref, x_ref, wg_hbm, wu_hbm, wd_hbm, o_ref,
                   buf_ref, sg_ref, su_ref, sd_ref, wg_ref, wu_ref, wd_ref, sem, wsem, *, layer):
    blk = pl.program_id(0)
    rows = buf_ref.shape[1]
    n_active = na_ref[0]

    @pl.when(blk == 0)
    def _():
        buf_ref[...] = jnp.zeros_like(buf_ref)

    def weight_copies(e):
        return (pltpu.make_async_copy(wg_hbm.at[layer, e], sg_ref, wsem.at[0]),
                pltpu.make_async_copy(wu_hbm.at[layer, e], su_ref, wsem.at[1]),
                pltpu.make_async_copy(wd_hbm.at[layer, e], sd_ref, wsem.at[2]))

    def row_copy(b, r):
        slot = b % 2
        tok = tok_ref[b * rows + r]
        return pltpu.make_async_copy(x_ref.at[pl.ds(tok, 1), :], buf_ref.at[slot, pl.ds(r, 1), :], sem.at[slot])

    def gather(b):
        def start(g, c):
            for j in range(GATHER_UNROLL):
                row_copy(b, g * GATHER_UNROLL + j).start()
            return c

        lax.fori_loop(0, grp_ref[b], start, 0)

    @pl.when(jnp.logical_and(blk == 0, n_active > 0))
    def _():
        for cp in weight_copies(be_ref[0]):
            cp.start(priority=1)
        gather(blk)

    @pl.when(blk + 1 < n_active)
    def _():
        gather(blk + 1)

    @pl.when(jnp.logical_and(blk < n_active, first_ref[blk] == 1))
    def _():
        for cp in weight_copies(be_ref[blk]):
            cp.wait()
        wg_ref[...] = sg_ref[...].astype(BF16)
        wu_ref[...] = su_ref[...].astype(BF16)
        wd_ref[...] = sd_ref[...].astype(BF16)

        @pl.when(next_ref[blk] >= 0)
        def _():
            for cp in weight_copies(next_ref[blk]):
                cp.start(priority=1)

    @pl.when(blk < n_active)
    def _():
        def wait(g, c):
            for j in range(GATHER_UNROLL):
                row_copy(blk, g * GATHER_UNROLL + j).wait()
            return c

        lax.fori_loop(0, grp_ref[blk], wait, 0)
        x = buf_ref[blk % 2].astype(BF16)
        gate = jnp.dot(x, wg_ref[...], preferred_element_type=F32)
        up = jnp.dot(x, wu_ref[...], preferred_element_type=F32)
        act = (gate * jax.nn.sigmoid(gate) * up).astype(BF16)
        o_ref[...] = jnp.dot(act, wd_ref[...], preferred_element_type=F32)

    @pl.when(blk >= na_ref[0])
    def _():
        o_ref[...] = jnp.zeros_like(o_ref)


def run_experts(block_expert, n_active, block_rows, row_token, x, w_gate, w_up, w_down, layer):
    n_blocks = block_expert.shape[0]
    _, _, d, ff = w_gate.shape
    rows = MOE_ROWS
    idx = jnp.arange(n_blocks, dtype=I32)
    first = jnp.concatenate([jnp.ones((1,), I32), (block_expert[1:] != block_expert[:-1]).astype(I32)])
    later = (idx[None, :] > idx[:, None]) & (block_expert[None, :] != block_expert[:, None]) & (idx[None, :] < n_active[0])
    nxt_blk = jnp.min(jnp.where(later, idx[None, :], n_blocks), axis=1)
    nxt = jnp.where(nxt_blk < n_blocks, block_expert[jnp.minimum(nxt_blk, n_blocks - 1)], -1).astype(I32)
    groups = (-(-block_rows // GATHER_UNROLL)).astype(I32)
    hbm = pl.BlockSpec(memory_space=pl.ANY)
    return pl.pallas_call(
        functools.partial(_expert_kernel, layer=layer),
        grid_spec=pltpu.PrefetchScalarGridSpec(
            num_scalar_prefetch=6,
            grid=(n_blocks,),
            in_specs=[hbm, hbm, hbm, hbm],
            out_specs=pl.BlockSpec((rows, d), lambda i, *_: (i, 0)),
            scratch_shapes=[pltpu.VMEM((2, rows, d), F32),
                            pltpu.VMEM((d, ff), F32), pltpu.VMEM((d, ff), F32), pltpu.VMEM((ff, d), F32),
                            pltpu.VMEM((d, ff), BF16), pltpu.VMEM((d, ff), BF16), pltpu.VMEM((ff, d), BF16),
                            pltpu.SemaphoreType.DMA((2,)), pltpu.SemaphoreType.DMA((3,))],
        ),
        out_shape=jax.ShapeDtypeStruct((n_blocks * rows, d), F32),
        compiler_params=_params("arbitrary"),
        name="run_experts",
    )(block_expert, n_active, first, nxt, groups, row_token, x, w_gate, w_up, w_down)


def _combine_kernel(dest_ref, h_ref, wt_ref, p_ref, gp_ref, wpg_ref, wpp_ref, gf_ref, y_ref, o_ref, buf_ref, sem):
    i = pl.program_id(0)
    tm = h_ref.shape[0]

    def row_copy(t, r, c):
        slot = t % 2
        src = dest_ref[(t * tm + r) * TOP_K + c]
        return pltpu.make_async_copy(y_ref.at[pl.ds(src, 1), :], buf_ref.at[slot, c, pl.ds(r, 1), :], sem.at[slot, c])

    def gather(t):
        def start(r, carry):
            for c in range(TOP_K):
                row_copy(t, r, c).start()
            return carry

        lax.fori_loop(0, tm, start, 0, unroll=GATHER_UNROLL)

    @pl.when(i == 0)
    def _():
        gather(i)

    @pl.when(i + 1 < pl.num_programs(0))
    def _():
        gather(i + 1)

    def wait(r, carry):
        for c in range(TOP_K):
            row_copy(i, r, c).wait()
        return carry

    lax.fori_loop(0, tm, wait, 0, unroll=GATHER_UNROLL)
    wt = wt_ref[...]
    slot = i % 2
    h2 = h_ref[...] + (wt[:, 0:1] * buf_ref[slot, 0] + wt[:, 1:2] * buf_ref[slot, 1])
    gate = jax.nn.sigmoid(jnp.dot(_rms(h2, gp_ref[...]).astype(BF16), wpg_ref[...], preferred_element_type=F32))
    ple = jnp.dot(p_ref[...].astype(BF16), wpp_ref[...], preferred_element_type=F32)
    o_ref[...] = _rms(h2 + gate * ple, gf_ref[...])


def combine_and_finish(dest, h1, wts, p, g_ple, w_ple_gate, w_ple_proj, g_final, y_rows):
    t, d = h1.shape
    tm = min(256, t)
    row = lambda n: pl.BlockSpec((tm, n), lambda i, ds: (i, 0))
    full = lambda a: pl.BlockSpec(a.shape, lambda i, ds: (0, 0))
    gp, gf = g_ple.reshape(1, d), g_final.reshape(1, d)
    return pl.pallas_call(
        _combine_kernel,
        grid_spec=pltpu.PrefetchScalarGridSpec(
            num_scalar_prefetch=1,
            grid=(t // tm,),
            in_specs=[row(d), row(LANES), row(p.shape[1]), full(gp), full(w_ple_gate), full(w_ple_proj), full(gf),
                      pl.BlockSpec(memory_space=pl.ANY)],
            out_specs=row(d),
            scratch_shapes=[pltpu.VMEM((2, TOP_K, tm, d), F32), pltpu.SemaphoreType.DMA((2, TOP_K))],
        ),
        out_shape=jax.ShapeDtypeStruct((t, d), F32),
        compiler_params=_params("arbitrary"),
        name="combine_and_finish",
    )(dest, h1, wts, p, gp, w_ple_gate, w_ple_proj, gf, y_rows)


def _segments(d):
    qk = d // 2
    sizes = dict(gq=qk, gk=qk, gv=d, gr=d, ga=GLA_RANK, fq=d, fk=d, fv=d, ff=d // FOX_DH, mg=d, mf=d)
    start, acc = {}, 0
    for name, size in sizes.items():
        start[name] = acc
        acc += size
    return sizes, start


def kernel(x_prompt, x_sample, cache_fox_k, cache_fox_v, cache_fox_logf, state_gla, page_table,
           p_prompt, p_sample, g_mix, w_in, w_gla_a2, b_gla_a, b_fox_f, g_gla_out, w_o_gla, w_o_fox, w_out,
           g_ffn, w_route_group, b_route_group, w_route_expert, b_route_expert, w_exp_gate, w_exp_up,
           w_exp_down, g_ple, w_ple_gate, w_ple_proj, g_final):
    depth = w_in.shape[0]
    bp, sp, d = x_prompt.shape
    bs, ss, _ = x_sample.shape
    assert ss == 1, "the sample group carries one new token per sequence"
    fox_heads = d // FOX_DH
    n_phys, page_size = cache_fox_k.shape[1:3]
    tp, ts = bp * sp, bs * ss
    sizes, start = _segments(d)
    main = ("gq", "gk", "gv", "gr", "fq", "mg", "mf", "fk", "fv")
    off, acc = {}, 0
    for name in main:
        off[name] = acc
        acc += sizes[name]

    hp = x_prompt.reshape(tp, d)
    hs = x_sample.reshape(ts, d)
    outs = {name: [] for name in ("kp", "vp", "lfp", "sp", "ks", "vs", "lfs", "ss")}
    for i in range(depth):
        w = w_in[i]
        seg = lambda name: w[:, start[name]:start[name] + sizes[name]]
        w_main = jnp.concatenate([seg(n) for n in main], axis=1).astype(BF16)
        w_small = jnp.pad(jnp.concatenate([seg("ga"), seg("ff")], axis=1),
                          ((0, 0), (0, LANES - GLA_RANK - fox_heads))).astype(BF16)
        w_a2_pad = jnp.pad(w_gla_a2[i], ((0, LANES - GLA_RANK), (0, 0))).astype(BF16)
        w_router = jnp.pad(jnp.concatenate([w_route_group[i], w_route_expert[i]], axis=1),
                           ((0, 0), (0, LANES - N_GROUPS - N_EXPERTS))).astype(BF16)
        b_router = jnp.pad(jnp.concatenate([b_route_group[i], b_route_expert[i]]),
                           (0, LANES - N_GROUPS - N_EXPERTS))
        wog, wof, wo = w_o_gla[i].astype(BF16), w_o_fox[i].astype(BF16), w_out[i].astype(BF16)

        def forget(z_small, batch, seq):
            f_t = z_small[:, GLA_RANK:GLA_RANK + fox_heads].reshape(batch, seq, fox_heads).transpose(0, 2, 1)
            return fox_log_forget(f_t, b_fox_f[i])

        us = rms_cast(hs, g_mix[i])
        zs, zss = matmul(us, w_main), matmul(us, w_small, tn=LANES)
        las = gla_log_decay(zss, w_a2_pad, b_gla_a[i])
        fks, fvs = zs[:, off["fk"]:off["fk"] + d], zs[:, off["fv"]:off["fv"] + d]
        lfs_t, _ = forget(zss, 1, ts)
        lfs = lfs_t[0].T
        ogs, s_new = gla_sample(zs, las, g_gla_out[i], state_gla[i], off)
        bias = fox_sample_bias(page_table, lfs.reshape(bs, fox_heads, 1),
                               cache_fox_logf[i].transpose(0, 2, 1))
        n_pages = page_table.shape[1]
        bias = bias.transpose(0, 1, 3, 2).reshape(bs, n_pages // FOX_PAGES_PER_STEP, FOX_PAGES_PER_STEP,
                                                  page_size * fox_heads)
        heads = lambda a: a.reshape(bs, fox_heads, FOX_DH)
        decode = (page_table, heads(zs[:, off["fq"]:off["fq"] + d]), heads(fks), heads(fvs), bias,
                  cache_fox_k, cache_fox_v, i)

        up = rms_cast(hp, g_mix[i])
        zsp = matmul(up, w_small, tn=LANES)
        if projection_fits_decode(page_table, up, w_main):
            ofs, zp = fox_sample(*decode, proj=(up, w_main))
        else:
            ofs, zp = fox_sample(*decode), matmul(up, w_main)
        ofs = ofs.reshape(bs, d)
        lf_t, c_t = forget(zsp, bp, sp)
        ogp, s_fin = gla_prompt(zp, zsp, w_a2_pad, b_gla_a[i], g_gla_out[i], bp, sp, off)
        ofp, fkp, fvp = fox_prompt(zp, c_t.reshape(bp * fox_heads, 1, sp), bp, sp, off, fox_heads)
        mixp = merge_branches(ogp, ofp, wog, wof, zp, off["mg"], off["mf"])
        h1p, hn_all, idp, wtp = out_proj_and_route(mixp, wo, hp, g_ffn[i], w_router, b_router,
                                                   jnp.zeros((tp + ts, d), F32))
        outs["kp"].append(fkp.reshape(bp, sp, fox_heads, FOX_DH))
        outs["vp"].append(fvp.reshape(bp, sp, fox_heads, FOX_DH))
        outs["lfp"].append(lf_t.transpose(0, 2, 1))
        outs["sp"].append(s_fin)

        mixs = merge_branches(ogs, ofs, wog, wof, zs, off["mg"], off["mf"])
        h1s, hn_all, ids, wts = out_proj_and_route(mixs, wo, hs, g_ffn[i], w_router, b_router, hn_all,
                                                   row_offset=tp)
        outs["ks"].append(fks.reshape(bs, ss, fox_heads, FOX_DH))
        outs["vs"].append(fvs.reshape(bs, ss, fox_heads, FOX_DH))
        outs["lfs"].append(lfs.reshape(bs, ss, fox_heads))
        outs["ss"].append(s_new)

        t_all = tp + ts
        n_slots = t_all * TOP_K
        t_pad = -(-t_all // 512) * 512
        ids_all = jnp.concatenate([idp, ids], axis=0)
        ids_pad = jnp.pad(ids_all, ((0, t_pad - t_all), (0, 0)), constant_values=LANES - 1)
        rank, counts = rank_slots(ids_pad)
        counts = counts[0, :N_EXPERTS].astype(I32)
        padded = -(-counts // MOE_ROWS) * MOE_ROWS
        ends = jnp.cumsum(padded)
        experts = ids_all[:, :TOP_K]
        dest = (ends - padded)[experts] + rank[:t_all, :TOP_K]
        n_blocks = -(-n_slots // MOE_ROWS) + N_EXPERTS
        n_active = (ends[-1] // MOE_ROWS).reshape(1)
        block_start = jnp.arange(n_blocks, dtype=I32) * MOE_ROWS
        block_expert = jnp.searchsorted(ends, jnp.minimum(block_start, ends[-1] - 1), side="right").astype(I32)
        tokens = jnp.broadcast_to(jnp.arange(t_all, dtype=I32)[:, None], (t_all, TOP_K))
        row_token = jnp.zeros((n_blocks * MOE_ROWS,), I32).at[dest.reshape(-1)].set(tokens.reshape(-1))
        block_rows = jnp.clip(counts[block_expert] - (block_start - (ends - padded)[block_expert]), 0, MOE_ROWS)
        y_rows = run_experts(block_expert, n_active.astype(I32), block_rows, row_token, hn_all,
                             w_exp_gate, w_exp_up, w_exp_down, i)

        last = i == depth - 1
        assert last, "the final norm is fused into the last layer's finishing kernel"
        fin = (g_ple[i], w_ple_gate[i].astype(BF16), w_ple_proj[i].astype(BF16), g_final, y_rows)
        hp = combine_and_finish(dest[:tp].reshape(-1), h1p, wtp, p_prompt[i].reshape(tp, -1), *fin)
        hs = combine_and_finish(dest[tp:].reshape(-1), h1s, wts, p_sample[i].reshape(ts, -1), *fin)

    stack = lambda name: jnp.stack(outs[name])
    return (hp.reshape(bp, sp, d), hs.reshape(bs, ss, d),
            stack("kp"), stack("vp"), stack("lfp"), stack("sp"),
            stack("ks"), stack("vs"), stack("lfs"), stack("ss"))
```
